```python
import jax, jax.numpy as jnp
from jax import lax
import numpy as np

D_MODEL = 1024
BATCH = 2
SEQ = 16384
DEPTH = 4
DEC_BATCH = 8
DEC_SEQ = 8192
PAST_LEN = 128

D_MIX = D_MODEL
HEAD_DIM = 64
D_CONV = D_MIX // 2
D_FOURIER = D_MIX - D_CONV
N_CONV_HEADS = D_CONV // HEAD_DIM
N_FOURIER_GROUPS = D_FOURIER // HEAD_DIM
CONV_WIDTH = 3
D_FF = 4 * D_MODEL
D_IN_PROJ = 3 * D_CONV + D_FOURIER
EPS = 1e-6

kernel_name = "hybrid_shortconv_fnet_encoder"


def rms_norm(x, g):
    xf = x.astype(jnp.float32)
    y = xf * lax.rsqrt(jnp.mean(xf * xf, axis=-1, keepdims=True) + EPS)
    return (y * g.astype(jnp.float32)).astype(x.dtype)


def head_rms_norm(u, g, n_heads):
    b, s, d = u.shape
    uf = u.astype(jnp.float32).reshape(b, s, n_heads, d // n_heads)
    uf = uf * lax.rsqrt(jnp.mean(uf * uf, axis=-1, keepdims=True) + EPS)
    return (uf.reshape(b, s, d) * g.astype(jnp.float32)).astype(u.dtype)


def short_conv_mixer(u, conv_w):
    gate_b, gate_c, v = jnp.split(u, 3, axis=-1)
    z = gate_c * v
    zp = jnp.pad(z, ((0, 0), (1, 1), (0, 0)))
    conv = zp[:, :-2] * conv_w[0] + zp[:, 1:-1] * conv_w[1] + zp[:, 2:] * conv_w[2]
    return gate_b * conv


def fourier_mixer(u):
    b, s, d = u.shape
    ug = u.astype(jnp.float32).reshape(b, s, N_FOURIER_GROUPS, HEAD_DIM)
    yg = jnp.fft.fftn(ug, axes=(1, 3), norm="ortho").real
    return yg.reshape(b, s, d).astype(u.dtype)


def encoder_layer(x, g_mix_pre, w_in, conv_w, g_conv_out, g_fourier_out, w_out,
                  g_mix_post, g_mlp_pre, w_up, w_down, g_mlp_post):
    h = rms_norm(x, g_mix_pre)
    p = jnp.einsum("bsd,de->bse", h, w_in)
    yc = short_conv_mixer(p[..., :3 * D_CONV], conv_w)
    yf = fourier_mixer(p[..., 3 * D_CONV:])
    merged = jnp.concatenate(
        [head_rms_norm(yc, g_conv_out, N_CONV_HEADS),
         head_rms_norm(yf, g_fourier_out, N_FOURIER_GROUPS)], axis=-1)
    m = jnp.einsum("bse,ed->bsd", merged, w_out)
    x = x + rms_norm(m, g_mix_post)
    h2 = rms_norm(x, g_mlp_pre)
    a = jax.nn.relu(jnp.einsum("bsd,df->bsf", h2, w_up))
    f = jnp.einsum("bsf,fd->bsd", a * a, w_down)
    return x + rms_norm(f, g_mlp_post)


def run_trunk(x, g_mix_pre, w_in, conv_w, g_conv_out, g_fourier_out, w_out,
              g_mix_post, g_mlp_pre, w_up, w_down, g_mlp_post):
    for l in range(DEPTH):
        x = encoder_layer(x, g_mix_pre[l], w_in[l], conv_w[l], g_conv_out[l],
                          g_fourier_out[l], w_out[l], g_mix_post[l], g_mlp_pre[l],
                          w_up[l], w_down[l], g_mlp_post[l])
    return x


def setup_inputs(seed: int = 0) -> dict:
    key = jax.random.key(seed)
    ks = jax.random.split(key, 13)
    f32 = jnp.float32

    def gain(k, d):
        return 1.0 + 0.05 * jax.random.normal(k, (DEPTH, d), f32)

    return {
        "x_prompt": jax.random.normal(ks[0], (BATCH, SEQ, D_MODEL), f32),
        "x_sample": jax.random.normal(ks[1], (DEC_BATCH, DEC_SEQ, D_MODEL), f32),
        "g_mix_pre": gain(ks[2], D_MODEL),
        "w_in": jax.random.normal(ks[3], (DEPTH, D_MODEL, D_IN_PROJ), f32) * D_MODEL ** -0.5,
        "conv_w": jax.random.normal(ks[4], (DEPTH, CONV_WIDTH, D_CONV), f32) * CONV_WIDTH ** -0.5,
        "g_conv_out": gain(ks[5], D_CONV),
        "g_fourier_out": gain(ks[6], D_FOURIER),
        "w_out": jax.random.normal(ks[7], (DEPTH, D_MIX, D_MODEL), f32) * D_MIX ** -0.5,
        "g_mix_post": gain(ks[8], D_MODEL),
        "g_mlp_pre": gain(ks[9], D_MODEL),
        "w_up": jax.random.normal(ks[10], (DEPTH, D_MODEL, D_FF), f32) * D_MODEL ** -0.5,
        "w_down": jax.random.normal(ks[11], (DEPTH, D_FF, D_MODEL), f32) * D_FF ** -0.5,
        "g_mlp_post": gain(ks[12], D_MODEL),
    }


def reference(x_prompt, x_sample, g_mix_pre, w_in, conv_w, g_conv_out, g_fourier_out,
              w_out, g_mix_post, g_mlp_pre, w_up, w_down, g_mlp_post):
    y_prompt = run_trunk(x_prompt, g_mix_pre, w_in, conv_w, g_conv_out, g_fourier_out,
                         w_out, g_mix_post, g_mlp_pre, w_up, w_down, g_mlp_post)
    y_sample = run_trunk(x_sample, g_mix_pre, w_in, conv_w, g_conv_out, g_fourier_out,
                         w_out, g_mix_post, g_mlp_pre, w_up, w_down, g_mlp_post)
    return (y_prompt, y_sample)
```

```python
import functools
import math

import jax
import jax.numpy as jnp
from jax import lax
from jax.experimental import pallas as pl
from jax.experimental.pallas import tpu as pltpu

HEAD_DIM = 64
EPS = 1e-6
F32 = jnp.float32
BF16 = jnp.bfloat16

V7X_MXU_DIM = 256
V7X_SUBLANES = 8
VMEM_LIMIT_BYTES = 56 * 1024 * 1024

SEQ_DFT_N1 = 128
TOKEN_TILE = 512
DFT_CHUNK = 8
FF_CHUNK = 1024


def _rms(x, g):
    ms = jnp.mean(x * x, axis=-1, keepdims=True)
    return x * lax.rsqrt(ms + EPS) * g


def _const_spec(block_shape, index):
    nd = len(index)
    return pl.BlockSpec(block_shape, lambda *_: index, pipeline_mode=pl.Buffered(1))


def _inproj_body(x_ref, g_ref, w_ref, gb_ref, z_ref, u_ref, *, dc):
    h = _rms(x_ref[...], g_ref[...]).astype(BF16)
    p = jnp.dot(h, w_ref[...], preferred_element_type=F32)
    gb_ref[...] = p[:, :dc]
    z_ref[...] = p[:, dc:2 * dc] * p[:, 2 * dc:3 * dc]
    u_ref[...] = p[:, 3 * dc:]


def _inproj(x, g, w, layer):
    b, s, d = x.shape
    dc = w.shape[-1] // 4
    t = TOKEN_TILE
    tok = lambda width: pl.BlockSpec((None, t, width), lambda bi, i: (bi, i, 0))
    out = jax.ShapeDtypeStruct((b, s, dc), F32)
    return pl.pallas_call(
        functools.partial(_inproj_body, dc=dc),
        grid=(b, s // t),
        in_specs=[tok(d),
                  _const_spec((None, 1, d), (layer, 0, 0)),
                  _const_spec((None, d, 4 * dc), (layer, 0, 0))],
        out_specs=[tok(dc), tok(dc), tok(dc)],
        out_shape=[out, out, out],
        compiler_params=pltpu.CompilerParams(
            dimension_semantics=("arbitrary", "arbitrary"),
            vmem_limit_bytes=VMEM_LIMIT_BYTES),
        name=f"inproj_s{s}",
    )(x, g, w)


def _dft1_body(u_ref, l_ref, g_ref, *, n1, c):
    for j in range(c):
        slab = u_ref[:, j, :].astype(BF16)
        o = jnp.dot(l_ref[j], slab, preferred_element_type=F32)
        g_ref[0, :, j, :] = o[:n1]
        g_ref[1, :, j, :] = o[n1:]


def _dft1(u, l1):
    b, s, cw = u.shape
    n2, _, n1 = l1.shape
    c = DFT_CHUNK
    u4 = u.reshape(b, n1, n2, cw)
    return pl.pallas_call(
        functools.partial(_dft1_body, n1=n1, c=c),
        grid=(b, n2 // c),
        in_specs=[pl.BlockSpec((None, n1, c, cw), lambda bi, j: (bi, 0, j, 0)),
                  pl.BlockSpec((c, 2 * n1, n1), lambda bi, j: (j, 0, 0))],
        out_specs=pl.BlockSpec((None, 2, n1, c, cw), lambda bi, j: (bi, 0, 0, j, 0)),
        out_shape=jax.ShapeDtypeStruct((b, 2, n1, n2, cw), F32),
        compiler_params=pltpu.CompilerParams(
            dimension_semantics=("arbitrary", "arbitrary"),
            vmem_limit_bytes=VMEM_LIMIT_BYTES),
        name=f"dft1_s{s}",
    )(u4, l1)


def _dft2_body(g_ref, l2_ref, cc_ref, cs_ref, y_ref, *, n2, c):
    for k in range(c):
        w = jnp.concatenate([g_ref[0, k], g_ref[1, k]], axis=0).astype(BF16)
        v = jnp.dot(l2_ref[...], w, preferred_element_type=F32)
        vr = v[:n2].astype(BF16)
        vi = v[n2:].astype(BF16)
        y = (jnp.dot(vr, cc_ref[...], preferred_element_type=F32)
             + jnp.dot(vi, cs_ref[...], preferred_element_type=F32))
        y_ref[:, k, :] = y


def _dft2(g, l2, cc, cs):
    b, _, n1, n2, cw = g.shape
    c = DFT_CHUNK
    y = pl.pallas_call(
        functools.partial(_dft2_body, n2=n2, c=c),
        grid=(b, n1 // c),
        in_specs=[pl.BlockSpec((None, 2, c, n2, cw), lambda bi, k: (bi, 0, k, 0, 0)),
                  _const_spec((2 * n2, 2 * n2), (0, 0)),
                  _const_spec((cw, cw), (0, 0)),
                  _const_spec((cw, cw), (0, 0))],
        out_specs=pl.BlockSpec((None, n2, c, cw), lambda bi, k: (bi, 0, k, 0)),
        out_shape=jax.ShapeDtypeStruct((b, n2, n1, cw), F32),
        compiler_params=pltpu.CompilerParams(
            dimension_semantics=("arbitrary", "arbitrary"),
            vmem_limit_bytes=VMEM_LIMIT_BYTES),
        name=f"dft2_s{n1 * n2}",
    )(g, l2, cc, cs)
    return y.reshape(b, n1 * n2, cw)


def _mix_mlp_body(x_ref, gb_ref, z_ref, zp_ref, zn_ref, yf_ref, cw_ref, gco_ref, gfo_ref,
                  hm_ref, wo_ref, gpost_ref, gpre2_ref, wu_ref, wd_ref, gpost2_ref, o_ref,
                  *, n_tiles, ff_chunk):
    i = pl.program_id(1)
    z = z_ref[...]
    t = z.shape[0]
    row = lax.broadcasted_iota(jnp.int32, z.shape, 0)
    halo = V7X_SUBLANES
    z_before = jnp.where(i > 0, zp_ref[halo - 1:halo, :], 0.0)
    z_after = jnp.where(i < n_tiles - 1, zn_ref[0:1, :], 0.0)
    z_m1 = jnp.where(row == 0, z_before, pltpu.roll(z, 1, 0))
    z_p1 = jnp.where(row == t - 1, z_after, pltpu.roll(z, t - 1, 0))
    cw = cw_ref[...]
    conv = z_m1 * cw[0:1] + z * cw[1:2] + z_p1 * cw[2:3]
    yc = gb_ref[...] * conv
    yf = yf_ref[...]

    def head_norm(y, g):
        ms = jnp.dot((y * y).astype(BF16), hm_ref[...], preferred_element_type=F32)
        return y * lax.rsqrt(ms + EPS) * g

    merged = jnp.concatenate([head_norm(yc, gco_ref[...]), head_norm(yf, gfo_ref[...])], axis=-1)
    m = jnp.dot(merged.astype(BF16), wo_ref[...], preferred_element_type=F32)
    x1 = x_ref[...] + _rms(m, gpost_ref[...])

    h2 = _rms(x1, gpre2_ref[...]).astype(BF16)
    d_ff = wu_ref.shape[-1]
    f = jnp.zeros_like(x1)
    for c0 in range(0, d_ff, ff_chunk):
        a = jnp.maximum(jnp.dot(h2, wu_ref[:, c0:c0 + ff_chunk], preferred_element_type=F32), 0.0)
        f = f + jnp.dot((a * a).astype(BF16), wd_ref[c0:c0 + ff_chunk, :], preferred_element_type=F32)
    o_ref[...] = x1 + _rms(f, gpost2_ref[...])


def _mix_mlp(x, gb, z, yf, conv_w, g_co, g_fo, head_mean, w_out, g_post, g_pre2, w_up, w_down,
             g_post2, layer):
    b, s, d = x.shape
    dc = gb.shape[-1]
    d_ff = w_up.shape[-1]
    t = TOKEN_TILE
    n_tiles = s // t
    halo = V7X_SUBLANES
    hpt = t // halo
    n_halo = s // halo
    tok = lambda width: pl.BlockSpec((None, t, width), lambda bi, i: (bi, i, 0))
    vec = lambda width: _const_spec((None, 1, width), (layer, 0, 0))
    return pl.pallas_call(
        functools.partial(_mix_mlp_body, n_tiles=n_tiles, ff_chunk=FF_CHUNK),
        grid=(b, n_tiles),
        in_specs=[tok(d), tok(dc), tok(dc),
                  pl.BlockSpec((None, halo, dc), lambda bi, i: (bi, jnp.maximum(i * hpt - 1, 0), 0)),
                  pl.BlockSpec((None, halo, dc), lambda bi, i: (bi, jnp.minimum((i + 1) * hpt, n_halo - 1), 0)),
                  tok(dc),
                  _const_spec((None, 3, dc), (layer, 0, 0)),
                  vec(dc), vec(dc),
                  _const_spec((dc, dc), (0, 0)),
                  _const_spec((None, d, d), (layer, 0, 0)),
                  vec(d), vec(d),
                  _const_spec((None, d, d_ff), (layer, 0, 0)),
                  _const_spec((None, d_ff, d), (layer, 0, 0)),
                  vec(d)],
        out_specs=tok(d),
        out_shape=jax.ShapeDtypeStruct((b, s, d), F32),
        compiler_params=pltpu.CompilerParams(
            dimension_semantics=("arbitrary", "arbitrary"),
            vmem_limit_bytes=VMEM_LIMIT_BYTES),
        name=f"mix_mlp_s{s}",
    )(x, gb, z, z, z, yf, conv_w, g_co, g_fo, head_mean, w_out, g_post, g_pre2, w_up, w_down, g_post2)


def _angles(m, period):
    theta = (m % period).astype(F32) * (2.0 * math.pi / period)
    return jnp.cos(theta), jnp.sin(theta)


def _seq_dft_tables(s, n1):
    n2 = s // n1
    k1 = lax.broadcasted_iota(jnp.int32, (n2, n1, n1), 1)
    a = lax.broadcasted_iota(jnp.int32, (n2, n1, n1), 2)
    j = lax.broadcasted_iota(jnp.int32, (n2, n1, n1), 0)
    c, sn = _angles(k1 * (n2 * a + j), s)
    scale1 = 1.0 / math.sqrt(n1)
    l1 = (jnp.concatenate([c, -sn], axis=1) * scale1).astype(BF16)
    k2 = lax.broadcasted_iota(jnp.int32, (n2, n2), 0)
    b = lax.broadcasted_iota(jnp.int32, (n2, n2), 1)
    c2, s2 = _angles(k2 * b, n2)
    scale2 = 1.0 / math.sqrt(n2)
    l2 = (jnp.block([[c2, s2], [-s2, c2]]) * scale2).astype(BF16)
    return l1, l2


def _channel_tables(width):
    r = lax.broadcasted_iota(jnp.int32, (width, width), 0)
    q = lax.broadcasted_iota(jnp.int32, (width, width), 1)
    same = (r // HEAD_DIM) == (q // HEAD_DIM)
    c, sn = _angles((r % HEAD_DIM) * (q % HEAD_DIM), HEAD_DIM)
    scale = 1.0 / math.sqrt(HEAD_DIM)
    cc = jnp.where(same, c * scale, 0.0).astype(BF16)
    cs = jnp.where(same, sn * scale, 0.0).astype(BF16)
    head_mean = jnp.where(same, 1.0 / HEAD_DIM, 0.0).astype(BF16)
    return cc, cs, head_mean


def _trunk(x, params, tables):
    (g_mix_pre, w_in, conv_w, g_conv_out, g_fourier_out, w_out, g_mix_post, g_mlp_pre, w_up,
     w_down, g_mlp_post) = params
    l1, l2, cc, cs, head_mean = tables
    depth = w_in.shape[0]
    for layer in range(depth):
        gb, z, u = _inproj(x, g_mix_pre, w_in, layer)
        yf = _dft2(_dft1(u, l1), l2, cc, cs)
        x = _mix_mlp(x, gb, z, yf, conv_w, g_conv_out, g_fourier_out, head_mean, w_out,
                     g_mix_post, g_mlp_pre, w_up, w_down, g_mlp_post, layer)
    return x


def kernel(x_prompt, x_sample, g_mix_pre, w_in, conv_w, g_conv_out, g_fourier_out, w_out,
           g_mix_post, g_mlp_pre, w_up, w_down, g_mlp_post):
    depth, d, _ = w_in.shape
    vec3 = lambda g: g.reshape(depth, 1, g.shape[-1])
    params = (vec3(g_mix_pre), w_in.astype(BF16), conv_w, vec3(g_conv_out), vec3(g_fourier_out),
              w_out.astype(BF16), vec3(g_mix_post), vec3(g_mlp_pre), w_up.astype(BF16),
              w_down.astype(BF16), vec3(g_mlp_post))
    cc, cs, head_mean = _channel_tables(w_in.shape[-1] // 4)
    outs = []
    for x in (x_prompt, x_sample):
        l1, l2 = _seq_dft_tables(x.shape[1], SEQ_DFT_N1)
        outs.append(_trunk(x, params, (l1, l2, cc, cs, head_mean)))
    return tuple(outs)
```

```python
import functools
import math

import jax
import jax.numpy as jnp
from jax import lax
from jax.experimental import pallas as pl
from jax.experimental.pallas import tpu as pltpu

HEAD_DIM = 64
EPS = 1e-6
F32 = jnp.float32
BF16 = jnp.bfloat16
U32 = jnp.uint32

V7X_LANES = 128
V7X_SUBLANES = 8
V7X_MXU_DIM = 256
VMEM_LIMIT_BYTES = 56 * 1024 * 1024

SEQ_DFT_N1 = 128
TOKEN_TILE = 512
DFT_CHUNK = V7X_SUBLANES
FF_CHUNK = 1024


def _rms(x, g):
    ms = jnp.mean(x * x, axis=-1, keepdims=True)
    return x * lax.rsqrt(ms + EPS) * g


def _const_spec(block_shape, index):
    return pl.BlockSpec(block_shape, lambda *_: index, pipeline_mode=pl.Buffered(1))


def _params():
    return pltpu.CompilerParams(dimension_semantics=("arbitrary", "arbitrary"),
                                vmem_limit_bytes=VMEM_LIMIT_BYTES)


def _lane_tile(x, t):
    return x[:, t * V7X_LANES:(t + 1) * V7X_LANES]


def _inproj_body(x_ref, g_ref, w_ref, gb_ref, z_ref, u_ref, *, dc):
    h = _rms(x_ref[...], g_ref[...]).astype(BF16)
    p = jnp.dot(h, w_ref[...], preferred_element_type=F32)
    gb_ref[...] = p[:, :dc]
    z_ref[...] = p[:, dc:2 * dc] * p[:, 2 * dc:3 * dc]
    u = p[:, 3 * dc:]
    for t in range(dc // V7X_LANES):
        u_ref[t] = _lane_tile(u, t)


def _inproj(x, g, w, layer):
    b, s, d = x.shape
    dc = w.shape[-1] // 4
    t = TOKEN_TILE
    n_lt = dc // V7X_LANES
    tok = lambda width: pl.BlockSpec((None, t, width), lambda bi, i: (bi, i, 0))
    out = jax.ShapeDtypeStruct((b, s, dc), F32)
    return pl.pallas_call(
        functools.partial(_inproj_body, dc=dc),
        grid=(b, s // t),
        in_specs=[tok(d),
                  _const_spec((None, 1, d), (layer, 0, 0)),
                  _const_spec((None, d, 4 * dc), (layer, 0, 0))],
        out_specs=[tok(dc), tok(dc),
                   pl.BlockSpec((None, n_lt, t, V7X_LANES), lambda bi, i: (bi, 0, i, 0))],
        out_shape=[out, out, jax.ShapeDtypeStruct((b, n_lt, s, V7X_LANES), F32)],
        compiler_params=_params(),
        name=f"inproj_s{s}",
    )(x, g, w)


def _dft1_body(u_ref, l_ref, g_ref, u2, g2, *, n1, c):
    n_lt = u_ref.shape[0]
    u2[...] = u_ref[...].reshape(u2.shape)
    for j in range(c):
        slab = jnp.concatenate([u2[t, pl.ds(j, n1, stride=c), :] for t in range(n_lt)], axis=1)
        o = jnp.dot(l_ref[j], slab.astype(BF16), preferred_element_type=F32)
        packed = pltpu.bitcast(o.astype(BF16), U32)
        for t in range(n_lt):
            g2[t, pl.ds(j, n1, stride=c), :] = _lane_tile(packed, t)
    g_ref[...] = g2[...].reshape(g_ref.shape)


def _dft1(u, l1):
    b, n_lt, s, _ = u.shape
    n2, _, n1 = l1.shape
    c = DFT_CHUNK
    blk = pl.BlockSpec((None, n_lt, n1, c, V7X_LANES), lambda bi, j: (bi, 0, 0, j, 0))
    return pl.pallas_call(
        functools.partial(_dft1_body, n1=n1, c=c),
        grid=(b, n2 // c),
        in_specs=[blk, pl.BlockSpec((c, 2 * n1, n1), lambda bi, j: (j, 0, 0))],
        out_specs=blk,
        out_shape=jax.ShapeDtypeStruct((b, n_lt, n1, n2, V7X_LANES), U32),
        scratch_shapes=[pltpu.VMEM((n_lt, n1 * c, V7X_LANES), F32),
                        pltpu.VMEM((n_lt, n1 * c, V7X_LANES), U32)],
        compiler_params=_params(),
        name=f"dft1_s{s}",
    )(u.reshape(b, n_lt, n1, n2, V7X_LANES), l1)


def _dft2_body(g_ref, l2_ref, cc_ref, cs_ref, y_ref, y2, *, n2, c):
    n_lt = g_ref.shape[0]
    vr, vi = [], []
    for k in range(c):
        w = jnp.concatenate([g_ref[t, k] for t in range(n_lt)], axis=1)
        v = jnp.dot(l2_ref[...], pltpu.bitcast(w, BF16), preferred_element_type=F32)
        vr.append(v[:n2].astype(BF16))
        vi.append(v[n2:].astype(BF16))
    vr = jnp.concatenate(vr, axis=0)
    vi = jnp.concatenate(vi, axis=0)
    group = cc_ref.shape[0]
    for q in range(n_lt * V7X_LANES // group):
        cols = slice(q * group, (q + 1) * group)
        y = (jnp.dot(vr[:, cols], cc_ref[...], preferred_element_type=F32)
             + jnp.dot(vi[:, cols], cs_ref[...], preferred_element_type=F32))
        for k in range(c):
            for t in range(group // V7X_LANES):
                y2[q * (group // V7X_LANES) + t, pl.ds(k, n2, stride=c), :] = _lane_tile(y[k * n2:(k + 1) * n2], t)
    y_ref[...] = y2[...].reshape(y_ref.shape)


def _dft2(g, l2, cc, cs):
    b, n_lt, n1, n2, _ = g.shape
    c = DFT_CHUNK
    y = pl.pallas_call(
        functools.partial(_dft2_body, n2=n2, c=c),
        grid=(b, n1 // c),
        in_specs=[pl.BlockSpec((None, n_lt, c, n2, V7X_LANES), lambda bi, k: (bi, 0, k, 0, 0)),
                  _const_spec(l2.shape, (0, 0)),
                  _const_spec(cc.shape, (0, 0)),
                  _const_spec(cs.shape, (0, 0))],
        out_specs=pl.BlockSpec((None, n_lt, n2, c, V7X_LANES), lambda bi, k: (bi, 0, 0, k, 0)),
        out_shape=jax.ShapeDtypeStruct((b, n_lt, n2, n1, V7X_LANES), F32),
        scratch_shapes=[pltpu.VMEM((n_lt, n2 * c, V7X_LANES), F32)],
        compiler_params=_params(),
        name=f"dft2_s{n1 * n2}",
    )(g, l2, cc, cs)
    return y.reshape(b, n_lt, n1 * n2, V7X_LANES)


def _mix_mlp_body(x_ref, gb_ref, z_ref, zp_ref, zn_ref, yf_ref, cw_ref, gco_ref, gfo_ref,
                  hm_ref, wo_ref, gpost_ref, gpre2_ref, wu_ref, wd_ref, gpost2_ref, o_ref,
                  *, n_tiles, ff_chunk):
    i = pl.program_id(1)
    z = z_ref[...]
    t = z.shape[0]
    row = lax.broadcasted_iota(jnp.int32, z.shape, 0)
    halo = V7X_SUBLANES
    z_before = jnp.where(i > 0, zp_ref[halo - 1:halo, :], 0.0)
    z_after = jnp.where(i < n_tiles - 1, zn_ref[0:1, :], 0.0)
    z_m1 = jnp.where(row == 0, z_before, pltpu.roll(z, 1, 0))
    z_p1 = jnp.where(row == t - 1, z_after, pltpu.roll(z, t - 1, 0))
    cw = cw_ref[...]
    conv = z_m1 * cw[0:1] + z * cw[1:2] + z_p1 * cw[2:3]
    yc = gb_ref[...] * conv
    yf = jnp.concatenate([yf_ref[k] for k in range(yf_ref.shape[0])], axis=1)

    def head_norm(y, g):
        ms = jnp.dot((y * y).astype(BF16), hm_ref[...], preferred_element_type=F32)
        return y * lax.rsqrt(ms + EPS) * g

    merged = jnp.concatenate([head_norm(yc, gco_ref[...]), head_norm(yf, gfo_ref[...])], axis=-1)
    m = jnp.dot(merged.astype(BF16), wo_ref[...], preferred_element_type=F32)
    x1 = x_ref[...] + _rms(m, gpost_ref[...])

    h2 = _rms(x1, gpre2_ref[...]).astype(BF16)
    d_ff = wu_ref.shape[-1]
    f = jnp.zeros_like(x1)
    for c0 in range(0, d_ff, ff_chunk):
        a = jnp.maximum(jnp.dot(h2, wu_ref[:, c0:c0 + ff_chunk], preferred_element_type=F32), 0.0)
        f = f + jnp.dot((a * a).astype(BF16), wd_ref[c0:c0 + ff_chunk, :], preferred_element_type=F32)
    o_ref[...] = x1 + _rms(f, gpost2_ref[...])


def _mix_mlp(x, gb, z, yf, conv_w, g_co, g_fo, head_mean, w_out, g_post, g_pre2, w_up, w_down,
             g_post2, layer):
    b, s, d = x.shape
    dc = gb.shape[-1]
    d_ff = w_up.shape[-1]
    t = TOKEN_TILE
    n_tiles = s // t
    halo = V7X_SUBLANES
    hpt = t // halo
    n_halo = s // halo
    tok = lambda width: pl.BlockSpec((None, t, width), lambda bi, i: (bi, i, 0))
    vec = lambda width: _const_spec((None, 1, width), (layer, 0, 0))
    return pl.pallas_call(
        functools.partial(_mix_mlp_body, n_tiles=n_tiles, ff_chunk=FF_CHUNK),
        grid=(b, n_tiles),
        in_specs=[tok(d), tok(dc), tok(dc),
                  pl.BlockSpec((None, halo, dc), lambda bi, i: (bi, jnp.maximum(i * hpt - 1, 0), 0)),
                  pl.BlockSpec((None, halo, dc), lambda bi, i: (bi, jnp.minimum((i + 1) * hpt, n_halo - 1), 0)),
                  pl.BlockSpec((None, yf.shape[1], t, V7X_LANES), lambda bi, i: (bi, 0, i, 0)),
                  _const_spec((None, 3, dc), (layer, 0, 0)),
                  vec(dc), vec(dc),
                  _const_spec((dc, dc), (0, 0)),
                  _const_spec((None, d, d), (layer, 0, 0)),
                  vec(d), vec(d),
                  _const_spec((None, d, d_ff), (layer, 0, 0)),
                  _const_spec((None, d_ff, d), (layer, 0, 0)),
                  vec(d)],
        out_specs=tok(d),
        out_shape=jax.ShapeDtypeStruct((b, s, d), F32),
        compiler_params=_params(),
        name=f"mix_mlp_s{s}",
    )(x, gb, z, z, z, yf, conv_w, g_co, g_fo, head_mean, w_out, g_post, g_pre2, w_up, w_down, g_post2)


def _angles(m, period):
    theta = (m % period).astype(F32) * (2.0 * math.pi / period)
    return jnp.cos(theta), jnp.sin(theta)


def _seq_dft_tables(s, n1):
    n2 = s // n1
    shape = (n2, n1, 2, n1)
    j = lax.broadcasted_iota(jnp.int32, shape, 0)
    k1 = lax.broadcasted_iota(jnp.int32, shape, 1)
    im = lax.broadcasted_iota(jnp.int32, shape, 2)
    a = lax.broadcasted_iota(jnp.int32, shape, 3)
    c, sn = _angles(k1 * (n2 * a + j), s)
    l1 = (jnp.where(im == 0, c, -sn) * (1.0 / math.sqrt(n1))).astype(BF16).reshape(n2, 2 * n1, n1)
    shape = (2, n2, n2, 2)
    out_im = lax.broadcasted_iota(jnp.int32, shape, 0)
    k2 = lax.broadcasted_iota(jnp.int32, shape, 1)
    b = lax.broadcasted_iota(jnp.int32, shape, 2)
    in_im = lax.broadcasted_iota(jnp.int32, shape, 3)
    c2, s2 = _angles(k2 * b, n2)
    l2 = jnp.where(out_im == in_im, c2, jnp.where(out_im == 0, s2, -s2))
    l2 = (l2 * (1.0 / math.sqrt(n2))).astype(BF16).reshape(2 * n2, 2 * n2)
    return l1, l2


def _channel_tables(width, group):
    def block_diag(n, fn):
        r = lax.broadcasted_iota(jnp.int32, (n, n), 0)
        q = lax.broadcasted_iota(jnp.int32, (n, n), 1)
        same = (r // HEAD_DIM) == (q // HEAD_DIM)
        return jnp.where(same, fn(r % HEAD_DIM, q % HEAD_DIM), 0.0).astype(BF16)

    scale = 1.0 / math.sqrt(HEAD_DIM)
    cc = block_diag(group, lambda r, q: _angles(r * q, HEAD_DIM)[0] * scale)
    cs = block_diag(group, lambda r, q: _angles(r * q, HEAD_DIM)[1] * scale)
    head_mean = block_diag(width, lambda r, q: jnp.full(r.shape, 1.0 / HEAD_DIM, F32))
    return cc, cs, head_mean


def _trunk(x, params, tables):
    (g_mix_pre, w_in, conv_w, g_conv_out, g_fourier_out, w_out, g_mix_post, g_mlp_pre, w_up,
     w_down, g_mlp_post) = params
    l1, l2, cc, cs, head_mean = tables
    depth = w_in.shape[0]
    for layer in range(depth):
        gb, z, u = _inproj(x, g_mix_pre, w_in, layer)
        yf = _dft2(_dft1(u, l1), l2, cc, cs)
        x = _mix_mlp(x, gb, z, yf, conv_w, g_conv_out, g_fourier_out, head_mean, w_out,
                     g_mix_post, g_mlp_pre, w_up, w_down, g_mlp_post, layer)
    return x


def kernel(x_prompt, x_sample, g_mix_pre, w_in, conv_w, g_conv_out, g_fourier_out, w_out,
           g_mix_post, g_mlp_pre, w_up, w_down, g_mlp_post):
    depth = w_in.shape[0]
    vec3 = lambda g: g.reshape(depth, 1, g.shape[-1])
    params = (vec3(g_mix_pre), w_in.astype(BF16), conv_w, vec3(g_conv_out), vec3(g_fourier_out),
              w_out.astype(BF16), vec3(g_mix_post), vec3(g_mlp_pre), w_up.astype(BF16),
              w_down.astype(BF16), vec3(g_mlp_post))
    cc, cs, head_mean = _channel_tables(w_in.shape[-1] // 4, V7X_MXU_DIM)
    outs = []
    for x in (x_prompt, x_sample):
        l1, l2 = _seq_dft_tables(x.shape[1], SEQ_DFT_N1)
        outs.append(_trunk(x, params, (l1, l2, cc, cs, head_mean)))
    return tuple(outs)
```

```python
import functools
import math

import jax
import jax.numpy as jnp
from jax import lax
from jax.experimental import pallas as pl
from jax.experimental.pallas import tpu as pltpu

HEAD_DIM = 64
EPS = 1e-6
F32 = jnp.float32
BF16 = jnp.bfloat16
U32 = jnp.uint32

V7X_LANES = 128
V7X_SUBLANES = 8
V7X_MXU_DIM = 256
VMEM_LIMIT_BYTES = 56 * 1024 * 1024

SEQ_DFT_N1 = 128
INPROJ_TILE = 1024
MIX_TILE = 512
DFT_CHUNK = V7X_SUBLANES
FF_CHUNK = 1024


def _rms(x, g):
    ms = jnp.mean(x * x, axis=-1, keepdims=True)
    return x * lax.rsqrt(ms + EPS) * g


def _const_spec(block_shape, index):
    return pl.BlockSpec(block_shape, lambda *_: index, pipeline_mode=pl.Buffered(1))


def _params():
    return pltpu.CompilerParams(dimension_semantics=("arbitrary", "arbitrary"),
                                vmem_limit_bytes=VMEM_LIMIT_BYTES)


def _lane_tile(x, t):
    return x[:, t * V7X_LANES:(t + 1) * V7X_LANES]


def _inproj_body(x_ref, g_ref, w_ref, gb_ref, z_ref, u_ref, *, dc):
    h = _rms(x_ref[...], g_ref[...]).astype(BF16)
    p = jnp.dot(h, w_ref[...], preferred_element_type=F32)
    gb_ref[...] = p[:, :dc]
    z_ref[...] = p[:, dc:2 * dc] * p[:, 2 * dc:3 * dc]
    u = p[:, 3 * dc:]
    for t in range(dc // V7X_LANES):
        u_ref[t] = _lane_tile(u, t)


def _inproj(x, g, w, layer):
    b, s, d = x.shape
    dc = w.shape[-1] // 4
    t = INPROJ_TILE
    n_lt = dc // V7X_LANES
    tok = lambda width: pl.BlockSpec((None, t, width), lambda bi, i: (bi, i, 0))
    out = jax.ShapeDtypeStruct((b, s, dc), F32)
    return pl.pallas_call(
        functools.partial(_inproj_body, dc=dc),
        grid=(b, s // t),
        in_specs=[tok(d),
                  _const_spec((None, 1, d), (layer, 0, 0)),
                  _const_spec((None, d, 4 * dc), (layer, 0, 0))],
        out_specs=[tok(dc), tok(dc),
                   pl.BlockSpec((None, n_lt, t, V7X_LANES), lambda bi, i: (bi, 0, i, 0))],
        out_shape=[out, out, jax.ShapeDtypeStruct((b, n_lt, s, V7X_LANES), F32)],
        compiler_params=_params(),
        name=f"inproj_s{s}",
    )(x, g, w)


def _seq_dft_body(u_ref, l1_ref, l2_ref, cc_ref, cs_ref, y_ref, g_scr, u2, y2, *, n1, n2, c):
    j = pl.program_id(1)
    n_chunks = n2 // c
    n_lt = u_ref.shape[0]

    @pl.when(j < n_chunks)
    def _():
        u2[...] = u_ref[...].reshape(u2.shape)
        for s in range(c):
            slab = jnp.concatenate([u2[t, pl.ds(s, n1, stride=c), :] for t in range(n_lt)], axis=1)
            o = jnp.dot(l1_ref[s], slab.astype(BF16), preferred_element_type=F32)
            packed = pltpu.bitcast(o.astype(BF16), U32)
            for t in range(n_lt):
                g_scr[t, j, pl.ds(s, n1, stride=c), :] = _lane_tile(packed, t)

    @pl.when(j >= n_chunks)
    def _():
        k0 = (j - n_chunks) * c
        vr, vi = [], []
        for k in range(c):
            rows = pl.ds((k0 + k) * c, c)
            w = jnp.concatenate([g_scr[t, :, rows, :].reshape(n2, V7X_LANES) for t in range(n_lt)], axis=1)
            v = jnp.dot(l2_ref[...], pltpu.bitcast(w, BF16), preferred_element_type=F32)
            vr.append(v[:n2].astype(BF16))
            vi.append(v[n2:].astype(BF16))
        vr = jnp.concatenate(vr, axis=0)
        vi = jnp.concatenate(vi, axis=0)
        group = cc_ref.shape[0]
        tiles = group // V7X_LANES
        for q in range(n_lt // tiles):
            cols = slice(q * group, (q + 1) * group)
            y = (jnp.dot(vr[:, cols], cc_ref[...], preferred_element_type=F32)
                 + jnp.dot(vi[:, cols], cs_ref[...], preferred_element_type=F32))
            for k in range(c):
                for t in range(tiles):
                    y2[q * tiles + t, pl.ds(k, n2, stride=c), :] = _lane_tile(y[k * n2:(k + 1) * n2], t)
        y_ref[...] = y2[...].reshape(y_ref.shape)


def _seq_dft(u, l1, l2, cc, cs):
    b, n_lt, s, _ = u.shape
    n2, _, n1 = l1.shape
    c = DFT_CHUNK
    n_chunks = n2 // c
    y = pl.pallas_call(
        functools.partial(_seq_dft_body, n1=n1, n2=n2, c=c),
        grid=(b, n_chunks + n1 // c),
        in_specs=[pl.BlockSpec((None, n_lt, n1, c, V7X_LANES),
                               lambda bi, j: (bi, 0, 0, jnp.minimum(j, n_chunks - 1), 0)),
                  pl.BlockSpec((c, 2 * n1, n1), lambda bi, j: (jnp.minimum(j, n_chunks - 1), 0, 0)),
                  _const_spec(l2.shape, (0, 0)),
                  _const_spec(cc.shape, (0, 0)),
                  _const_spec(cs.shape, (0, 0))],
        out_specs=pl.BlockSpec((None, n_lt, n2, c, V7X_LANES),
                               lambda bi, j: (bi, 0, 0, jnp.maximum(j - n_chunks, 0), 0)),
        out_shape=jax.ShapeDtypeStruct((b, n_lt, n2, n1, V7X_LANES), F32),
        scratch_shapes=[pltpu.VMEM((n_lt, n_chunks, n1 * c, V7X_LANES), U32),
                        pltpu.VMEM((n_lt, n1 * c, V7X_LANES), F32),
                        pltpu.VMEM((n_lt, n2 * c, V7X_LANES), F32)],
        compiler_params=_params(),
        name=f"seq_dft_s{s}",
    )(u.reshape(b, n_lt, n1, n2, V7X_LANES), l1, l2, cc, cs)
    return y.reshape(b, n_lt, s, V7X_LANES)


def _mix_mlp_body(x_ref, gb_ref, z_ref, zp_ref, zn_ref, yf_ref, cw_ref, gco_ref, gfo_ref,
                  hm_ref, wo_ref, gpost_ref, gpre2_ref, wu_ref, wd_ref, gpost2_ref, o_ref,
                  *, n_tiles, ff_chunk):
    i = pl.program_id(1)
    z = z_ref[...]
    t = z.shape[0]
    row = lax.broadcasted_iota(jnp.int32, z.shape, 0)
    halo = V7X_SUBLANES
    z_before = jnp.where(i > 0, zp_ref[halo - 1:halo, :], 0.0)
    z_after = jnp.where(i < n_tiles - 1, zn_ref[0:1, :], 0.0)
    z_m1 = jnp.where(row == 0, z_before, pltpu.roll(z, 1, 0))
    z_p1 = jnp.where(row == t - 1, z_after, pltpu.roll(z, t - 1, 0))
    cw = cw_ref[...]
    conv = z_m1 * cw[0:1] + z * cw[1:2] + z_p1 * cw[2:3]
    yc = gb_ref[...] * conv
    yf = jnp.concatenate([yf_ref[k] for k in range(yf_ref.shape[0])], axis=1)

    def head_norm(y, g):
        sq = (y * y).astype(BF16)
        group = hm_ref.shape[0]
        ms = jnp.concatenate(
            [jnp.dot(sq[:, q * group:(q + 1) * group], hm_ref[...], preferred_element_type=F32)
             for q in range(y.shape[1] // group)], axis=1)
        return y * lax.rsqrt(ms + EPS) * g

    merged = jnp.concatenate([head_norm(yc, gco_ref[...]), head_norm(yf, gfo_ref[...])], axis=-1)
    m = jnp.dot(merged.astype(BF16), wo_ref[...], preferred_element_type=F32)
    x1 = x_ref[...] + _rms(m, gpost_ref[...])

    h2 = _rms(x1, gpre2_ref[...]).astype(BF16)
    d_ff = wu_ref.shape[-1]
    f = jnp.zeros_like(x1)
    for c0 in range(0, d_ff, ff_chunk):
        a = jnp.maximum(jnp.dot(h2, wu_ref[:, c0:c0 + ff_chunk], preferred_element_type=F32), 0.0)
        f = f + jnp.dot((a * a).astype(BF16), wd_ref[c0:c0 + ff_chunk, :], preferred_element_type=F32)
    o_ref[...] = x1 + _rms(f, gpost2_ref[...])


def _mix_mlp(x, gb, z, yf, conv_w, g_co, g_fo, head_mean, w_out, g_post, g_pre2, w_up, w_down,
             g_post2, layer):
    b, s, d = x.shape
    dc = gb.shape[-1]
    d_ff = w_up.shape[-1]
    t = MIX_TILE
    n_tiles = s // t
    halo = V7X_SUBLANES
    hpt = t // halo
    n_halo = s // halo
    tok = lambda width: pl.BlockSpec((None, t, width), lambda bi, i: (bi, i, 0))
    vec = lambda width: _const_spec((None, 1, width), (layer, 0, 0))
    return pl.pallas_call(
        functools.partial(_mix_mlp_body, n_tiles=n_tiles, ff_chunk=FF_CHUNK),
        grid=(b, n_tiles),
        in_specs=[tok(d), tok(dc), tok(dc),
                  pl.BlockSpec((None, halo, dc), lambda bi, i: (bi, jnp.maximum(i * hpt - 1, 0), 0)),
                  pl.BlockSpec((None, halo, dc), lambda bi, i: (bi, jnp.minimum((i + 1) * hpt, n_halo - 1), 0)),
                  pl.BlockSpec((None, yf.shape[1], t, V7X_LANES), lambda bi, i: (bi, 0, i, 0)),
                  _const_spec((None, 3, dc), (layer, 0, 0)),
                  vec(dc), vec(dc),
                  _const_spec(head_mean.shape, (0, 0)),
                  _const_spec((None, d, d), (layer, 0, 0)),
                  vec(d), vec(d),
                  _const_spec((None, d, d_ff), (layer, 0, 0)),
                  _const_spec((None, d_ff, d), (layer, 0, 0)),
                  vec(d)],
        out_specs=tok(d),
        out_shape=jax.ShapeDtypeStruct((b, s, d), F32),
        compiler_params=_params(),
        name=f"mix_mlp_s{s}",
    )(x, gb, z, z, z, yf, conv_w, g_co, g_fo, head_mean, w_out, g_post, g_pre2, w_up, w_down, g_post2)


def _angles(m, period):
    theta = (m % period).astype(F32) * (2.0 * math.pi / period)
    return jnp.cos(theta), jnp.sin(theta)


def _seq_dft_tables(s, n1):
    n2 = s // n1
    shape = (n2, n1, 2, n1)
    j = lax.broadcasted_iota(jnp.int32, shape, 0)
    k1 = lax.broadcasted_iota(jnp.int32, shape, 1)
    im = lax.broadcasted_iota(jnp.int32, shape, 2)
    a = lax.broadcasted_iota(jnp.int32, shape, 3)
    c, sn = _angles(k1 * (n2 * a + j), s)
    l1 = (jnp.where(im == 0, c, -sn) * (1.0 / math.sqrt(n1))).astype(BF16).reshape(n2, 2 * n1, n1)
    shape = (2, n2, n2, 2)
    out_im = lax.broadcasted_iota(jnp.int32, shape, 0)
    k2 = lax.broadcasted_iota(jnp.int32, shape, 1)
    b = lax.broadcasted_iota(jnp.int32, shape, 2)
    in_im = lax.broadcasted_iota(jnp.int32, shape, 3)
    c2, s2 = _angles(k2 * b, n2)
    l2 = jnp.where(out_im == in_im, c2, jnp.where(out_im == 0, s2, -s2))
    l2 = (l2 * (1.0 / math.sqrt(n2))).astype(BF16).reshape(2 * n2, 2 * n2)
    return l1, l2


def _channel_tables(group):
    def block_diag(n, fn):
        r = lax.broadcasted_iota(jnp.int32, (n, n), 0)
        q = lax.broadcasted_iota(jnp.int32, (n, n), 1)
        same = (r // HEAD_DIM) == (q // HEAD_DIM)
        return jnp.where(same, fn(r % HEAD_DIM, q % HEAD_DIM), 0.0).astype(BF16)

    scale = 1.0 / math.sqrt(HEAD_DIM)
    cc = block_diag(group, lambda r, q: _angles(r * q, HEAD_DIM)[0] * scale)
    cs = block_diag(group, lambda r, q: _angles(r * q, HEAD_DIM)[1] * scale)
    head_mean = block_diag(group, lambda r, q: jnp.full(r.shape, 1.0 / HEAD_DIM, F32))
    return cc, cs, head_mean


def _trunk(x, params, tables):
    (g_mix_pre, w_in, conv_w, g_conv_out, g_fourier_out, w_out, g_mix_post, g_mlp_pre, w_up,
     w_down, g_mlp_post) = params
    l1, l2, cc, cs, head_mean = tables
    depth = w_in.shape[0]
    for layer in range(depth):
        gb, z, u = _inproj(x, g_mix_pre, w_in, layer)
        yf = _seq_dft(u, l1, l2, cc, cs)
        x = _mix_mlp(x, gb, z, yf, conv_w, g_conv_out, g_fourier_out, head_mean, w_out,
                     g_mix_post, g_mlp_pre, w_up, w_down, g_mlp_post, layer)
    return x


def kernel(x_prompt, x_sample, g_mix_pre, w_in, conv_w, g_conv_out, g_fourier_out, w_out,
           g_mix_post, g_mlp_pre, w_up, w_down, g_mlp_post):
    depth = w_in.shape[0]
    vec3 = lambda g: g.reshape(depth, 1, g.shape[-1])
    params = (vec3(g_mix_pre), w_in.astype(BF16), conv_w, vec3(g_conv_out), vec3(g_fourier_out),
              w_out.astype(BF16), vec3(g_mix_post), vec3(g_mlp_pre), w_up.astype(BF16),
              w_down.astype(BF16), vec3(g_mlp_post))
    cc, cs, head_mean = _channel_tables(V7X_MXU_DIM)
    outs = []
    for x in (x_prompt, x_sample):
        l1, l2 = _seq_dft_tables(x.shape[1], SEQ_DFT_N1)
        outs.append(_trunk(x, params, (l1, l2, cc, cs, head_mean)))
    return tuple(outs)
```

```python
import functools
import math

import jax
import jax.numpy as jnp
import numpy as np
from jax import lax
from jax.experimental import pallas as pl
from jax.experimental.pallas import tpu as pltpu

HEAD_DIM = 64
EPS = 1e-6
F32 = jnp.float32
BF16 = jnp.bfloat16
U32 = jnp.uint32

V7X_LANES = 128
V7X_SUBLANES = 8
V7X_MXU_DIM = 256
VMEM_LIMIT_BYTES = 56 * 1024 * 1024

SEQ_DFT_N1 = 128
INPROJ_TILE = 1024
MIX_TILE = 1024
MIX_SUB_ROWS = 256
DFT_CHUNK = V7X_SUBLANES
DFT_MAX_CHUNKS_PER_STEP = 2
FF_CHUNK = 1024


def _rms(x, g):
    ms = jnp.mean(x * x, axis=-1, keepdims=True)
    return x * lax.rsqrt(ms + EPS) * g


def _const_spec(block_shape, index):
    return pl.BlockSpec(block_shape, lambda *_: index, pipeline_mode=pl.Buffered(1))


def _params():
    return pltpu.CompilerParams(dimension_semantics=("arbitrary", "arbitrary"),
                                vmem_limit_bytes=VMEM_LIMIT_BYTES)


def _lane_tile(x, t):
    return x[:, t * V7X_LANES:(t + 1) * V7X_LANES]


def _inproj_body(x_ref, g_ref, w_ref, gb_ref, z_ref, u_ref, *, dc, n2):
    h = _rms(x_ref[...], g_ref[...]).astype(BF16)
    p = jnp.dot(h, w_ref[...], preferred_element_type=F32)
    gb_ref[...] = p[:, :dc]
    z_ref[...] = p[:, dc:2 * dc] * p[:, 2 * dc:3 * dc]
    u = p[:, 3 * dc:]
    n_chunks, _, rows, _ = u_ref.shape
    c = DFT_CHUNK
    for t in range(dc // V7X_LANES):
        ut = _lane_tile(u, t).reshape(rows // c, n_chunks, c, V7X_LANES)
        for j in range(n_chunks):
            u_ref[j, t] = ut[:, j].reshape(rows, V7X_LANES)


def _inproj(x, g, w, layer, n2):
    b, s, d = x.shape
    dc = w.shape[-1] // 4
    t = INPROJ_TILE
    n_lt = dc // V7X_LANES
    c = DFT_CHUNK
    n_chunks = n2 // c
    rows = (t // n2) * c
    tok = lambda width: pl.BlockSpec((None, t, width), lambda bi, i: (bi, i, 0))
    out = jax.ShapeDtypeStruct((b, s, dc), F32)
    return pl.pallas_call(
        functools.partial(_inproj_body, dc=dc, n2=n2),
        grid=(b, s // t),
        in_specs=[tok(d),
                  _const_spec((None, 1, d), (layer, 0, 0)),
                  _const_spec((None, d, 4 * dc), (layer, 0, 0))],
        out_specs=[tok(dc), tok(dc),
                   pl.BlockSpec((None, n_chunks, n_lt, rows, V7X_LANES), lambda bi, i: (bi, 0, 0, i, 0))],
        out_shape=[out, out, jax.ShapeDtypeStruct((b, n_chunks, n_lt, (s // n2) * c, V7X_LANES), F32)],
        compiler_params=_params(),
        name=f"inproj_s{s}",
    )(x, g, w)


def _seq_dft_body(u_ref, l1_ref, l2_ref, cc_ref, cs_ref, y_ref, g_scr, *, n1, n2, c, p):
    j = pl.program_id(1)
    steps1 = n2 // (c * p)
    n_lt = u_ref.shape[1]

    @pl.when(j < steps1)
    def _():
        for q in range(p):
            for s in range(c):
                slab = jnp.concatenate([u_ref[q, t, pl.ds(s, n1, stride=c), :] for t in range(n_lt)], axis=1)
                o = jnp.dot(l1_ref[q * c + s], slab.astype(BF16), preferred_element_type=F32)
                packed = pltpu.bitcast(o.astype(BF16), U32)
                for t in range(n_lt):
                    g_scr[t, j * p + q, pl.ds(s, n1, stride=c), :] = _lane_tile(packed, t)

    @pl.when(j >= steps1)
    def _():
        k0 = (j - steps1) * (p * c)
        vr, vi = [], []
        for k in range(p * c):
            rows = pl.ds((k0 + k) * c, c)
            w = jnp.concatenate([g_scr[t, :, rows, :].reshape(n2, V7X_LANES) for t in range(n_lt)], axis=1)
            v = jnp.dot(l2_ref[...], pltpu.bitcast(w, BF16), preferred_element_type=F32)
            vr.append(v[:n2].astype(BF16))
            vi.append(v[n2:].astype(BF16))
        vr = jnp.concatenate(vr, axis=0)
        vi = jnp.concatenate(vi, axis=0)
        group = cc_ref.shape[0]
        tiles = group // V7X_LANES
        for g in range(n_lt // tiles):
            cols = slice(g * group, (g + 1) * group)
            y = (jnp.dot(vr[:, cols], cc_ref[...], preferred_element_type=F32)
                 + jnp.dot(vi[:, cols], cs_ref[...], preferred_element_type=F32))
            for q in range(p):
                for k in range(c):
                    yk = y[(q * c + k) * n2:(q * c + k + 1) * n2]
                    for t in range(tiles):
                        y_ref[q, g * tiles + t, pl.ds(k, n2, stride=c), :] = _lane_tile(yk, t)


def _dft_chunks_per_step(n1, n2, width):
    c = DFT_CHUNK
    g_bytes = n1 * n2 * width * 4
    for p in range(DFT_MAX_CHUNKS_PER_STEP, 1, -1):
        io_bytes = 2 * p * c * ((n1 + n2) * width * 4 + 2 * n1 * n1 * 2)
        temp_bytes = p * c * n2 * width * (2 + 2 + 4) + 2 * n1 * width * 4 * 2
        if g_bytes + io_bytes + temp_bytes <= VMEM_LIMIT_BYTES and n2 % (p * c) == 0 and n1 % (p * c) == 0:
            return p
    return 1


def _seq_dft(u, l1, l2, cc, cs):
    b, n_chunks, n_lt, _, _ = u.shape
    n2, _, n1 = l1.shape
    c = DFT_CHUNK
    p = _dft_chunks_per_step(n1, n2, n_lt * V7X_LANES)
    steps1 = n_chunks // p
    steps2 = n1 // (c * p)
    return pl.pallas_call(
        functools.partial(_seq_dft_body, n1=n1, n2=n2, c=c, p=p),
        grid=(b, steps1 + steps2),
        in_specs=[pl.BlockSpec((None, p, n_lt, n1 * c, V7X_LANES),
                               lambda bi, j: (bi, jnp.minimum(j, steps1 - 1), 0, 0, 0)),
                  pl.BlockSpec((p * c, 2 * n1, n1), lambda bi, j: (jnp.minimum(j, steps1 - 1), 0, 0)),
                  _const_spec(l2.shape, (0, 0)),
                  _const_spec(cc.shape, (0, 0)),
                  _const_spec(cs.shape, (0, 0))],
        out_specs=pl.BlockSpec((None, p, n_lt, n2 * c, V7X_LANES),
                               lambda bi, j: (bi, jnp.maximum(j - steps1, 0), 0, 0, 0)),
        out_shape=jax.ShapeDtypeStruct((b, n1 // c, n_lt, n2 * c, V7X_LANES), F32),
        scratch_shapes=[pltpu.VMEM((n_lt, n_chunks, n1 * c, V7X_LANES), U32)],
        compiler_params=_params(),
        name=f"seq_dft_s{n1 * n2}",
    )(u, l1, l2, cc, cs)


def _mix_mlp_body(x_ref, gb_ref, z_ref, zp_ref, zn_ref, yf_ref, cw_ref, gco_ref, gfo_ref,
                  hm_ref, wo_ref, gpost_ref, gpre2_ref, wu_ref, wd_ref, gpost2_ref, o_ref,
                  *, n_tiles, sub_rows, ff_chunk):
    i = pl.program_id(1)
    r = sub_rows
    n_sub = z_ref.shape[0] // r
    n_ff = wu_ref.shape[-1] // ff_chunk
    halo = V7X_SUBLANES

    def head_norm(y, g):
        sq = (y * y).astype(BF16)
        group = hm_ref.shape[0]
        ms = jnp.concatenate(
            [jnp.dot(sq[:, q * group:(q + 1) * group], hm_ref[...], preferred_element_type=F32)
             for q in range(y.shape[1] // group)], axis=1)
        return y * lax.rsqrt(ms + EPS) * g

    def mix_inputs(s):
        rows = slice(s * r, (s + 1) * r)
        z = z_ref[rows, :]
        row = lax.broadcasted_iota(jnp.int32, z.shape, 0)
        if s == 0:
            z_before = jnp.where(i > 0, zp_ref[halo - 1:halo, :], 0.0)
        else:
            z_before = z_ref[s * r - 1:s * r, :]
        if s == n_sub - 1:
            z_after = jnp.where(i < n_tiles - 1, zn_ref[0:1, :], 0.0)
        else:
            z_after = z_ref[(s + 1) * r:(s + 1) * r + 1, :]
        z_m1 = jnp.where(row == 0, z_before, pltpu.roll(z, 1, 0))
        z_p1 = jnp.where(row == r - 1, z_after, pltpu.roll(z, r - 1, 0))
        cw = cw_ref[...]
        conv = z_m1 * cw[0:1] + z * cw[1:2] + z_p1 * cw[2:3]
        yc = gb_ref[rows, :] * conv
        n_kchunks, n_lt = yf_ref.shape[:2]
        c = DFT_CHUNK
        k2_per_sub = r // (n_kchunks * c)
        yf = jnp.concatenate(
            [jnp.concatenate([yf_ref[kc, t, k2 * c:(k2 + 1) * c, :]
                              for k2 in range(s * k2_per_sub, (s + 1) * k2_per_sub)
                              for kc in range(n_kchunks)], axis=0)
             for t in range(n_lt)], axis=1)
        merged = jnp.concatenate([head_norm(yc, gco_ref[...]), head_norm(yf, gfo_ref[...])], axis=-1)
        return merged.astype(BF16)

    def mix_out(s, merged):
        m = jnp.dot(merged, wo_ref[...], preferred_element_type=F32)
        x1 = x_ref[s * r:(s + 1) * r, :] + _rms(m, gpost_ref[...])
        return x1, _rms(x1, gpre2_ref[...]).astype(BF16)

    def mlp_chunk(h2, c):
        cols = slice(c * ff_chunk, (c + 1) * ff_chunk)
        a = jnp.maximum(jnp.dot(h2, wu_ref[:, cols], preferred_element_type=F32), 0.0)
        return jnp.dot((a * a).astype(BF16), wd_ref[cols, :], preferred_element_type=F32)

    x1, h2 = mix_out(0, mix_inputs(0))
    for s in range(n_sub):
        f = mlp_chunk(h2, 0)
        if s + 1 < n_sub:
            merged = mix_inputs(s + 1)
        f = f + mlp_chunk(h2, 1)
        if s + 1 < n_sub:
            nxt = mix_out(s + 1, merged)
        for c in range(2, n_ff):
            f = f + mlp_chunk(h2, c)
        o_ref[s * r:(s + 1) * r, :] = x1 + _rms(f, gpost2_ref[...])
        if s + 1 < n_sub:
            x1, h2 = nxt


def _mix_mlp(x, gb, z, yf, conv_w, g_co, g_fo, head_mean, w_out, g_post, g_pre2, w_up, w_down,
             g_post2, layer):
    b, s, d = x.shape
    dc = gb.shape[-1]
    d_ff = w_up.shape[-1]
    t = MIX_TILE
    n_tiles = s // t
    halo = V7X_SUBLANES
    hpt = t // halo
    n_halo = s // halo
    n_kchunks, n_lt = yf.shape[1:3]
    yf_rows = t // n_kchunks
    tok = lambda width: pl.BlockSpec((None, t, width), lambda bi, i: (bi, i, 0))
    vec = lambda width: _const_spec((None, 1, width), (layer, 0, 0))
    return pl.pallas_call(
        functools.partial(_mix_mlp_body, n_tiles=n_tiles, sub_rows=MIX_SUB_ROWS, ff_chunk=FF_CHUNK),
        grid=(b, n_tiles),
        in_specs=[tok(d), tok(dc), tok(dc),
                  pl.BlockSpec((None, halo, dc), lambda bi, i: (bi, jnp.maximum(i * hpt - 1, 0), 0)),
                  pl.BlockSpec((None, halo, dc), lambda bi, i: (bi, jnp.minimum((i + 1) * hpt, n_halo - 1), 0)),
                  pl.BlockSpec((None, n_kchunks, n_lt, yf_rows, V7X_LANES), lambda bi, i: (bi, 0, 0, i, 0)),
                  _const_spec((None, 3, dc), (layer, 0, 0)),
                  vec(dc), vec(dc),
                  _const_spec(head_mean.shape, (0, 0)),
                  _const_spec((None, d, d), (layer, 0, 0)),
                  vec(d), vec(d),
                  _const_spec((None, d, d_ff), (layer, 0, 0)),
                  _const_spec((None, d_ff, d), (layer, 0, 0)),
                  vec(d)],
        out_specs=tok(d),
        out_shape=jax.ShapeDtypeStruct((b, s, d), F32),
        compiler_params=_params(),
        name=f"mix_mlp_s{s}",
    )(x, gb, z, z, z, yf, conv_w, g_co, g_fo, head_mean, w_out, g_post, g_pre2, w_up, w_down, g_post2)


def _cos_sin(rows, cols, period):
    m = np.outer(np.arange(rows), np.arange(cols)) % period
    theta = m * (2.0 * np.pi / period)
    return np.cos(theta), np.sin(theta)


def _seq_dft_tables(s, n1):
    n2 = s // n1
    ca, sa = (jnp.asarray(t * (1.0 / math.sqrt(n1)), F32) for t in _cos_sin(n1, n1, n1))
    cb, sb = (jnp.asarray(t, F32)[:, :, None] for t in _cos_sin(n2, n1, s))
    re = ca[None] * cb - sa[None] * sb
    im = -(sa[None] * cb + ca[None] * sb)
    l1 = jnp.stack([re, im], axis=2).astype(BF16).reshape(n2, 2 * n1, n1)
    c2, s2 = _cos_sin(n2, n2, n2)
    l2 = np.stack([np.stack([c2, s2], axis=-1), np.stack([-s2, c2], axis=-1)], axis=0)
    l2 = jnp.asarray(l2.reshape(2 * n2, 2 * n2) * (1.0 / math.sqrt(n2)), BF16)
    return l1, l2


def _channel_tables(group):
    same = np.equal.outer(np.arange(group) // HEAD_DIM, np.arange(group) // HEAD_DIM)
    c, sn = _cos_sin(group, group, HEAD_DIM)
    scale = 1.0 / math.sqrt(HEAD_DIM)
    cc = jnp.asarray(np.where(same, c * scale, 0.0), BF16)
    cs = jnp.asarray(np.where(same, sn * scale, 0.0), BF16)
    head_mean = jnp.asarray(np.where(same, 1.0 / HEAD_DIM, 0.0), BF16)
    return cc, cs, head_mean


def _trunk(x, params, tables):
    (g_mix_pre, w_in, conv_w, g_conv_out, g_fourier_out, w_out, g_mix_post, g_mlp_pre, w_up,
     w_down, g_mlp_post) = params
    l1, l2, cc, cs, head_mean = tables
    depth = w_in.shape[0]
    for layer in range(depth):
        gb, z, u = _inproj(x, g_mix_pre, w_in, layer, l1.shape[0])
        yf = _seq_dft(u, l1, l2, cc, cs)
        x = _mix_mlp(x, gb, z, yf, conv_w, g_conv_out, g_fourier_out, head_mean, w_out,
                     g_mix_post, g_mlp_pre, w_up, w_down, g_mlp_post, layer)
    return x


def kernel(x_prompt, x_sample, g_mix_pre, w_in, conv_w, g_conv_out, g_fourier_out, w_out,
           g_mix_post, g_mlp_pre, w_up, w_down, g_mlp_post):
    depth = w_in.shape[0]
    vec3 = lambda g: g.reshape(depth, 1, g.shape[-1])
    params = (vec3(g_mix_pre), w_in.astype(BF16), conv_w, vec3(g_conv_out), vec3(g_fourier_out),
              w_out.astype(BF16), vec3(g_mix_post), vec3(g_mlp_pre), w_up.astype(BF16),
              w_down.astype(BF16), vec3(g_mlp_post))
    cc, cs, head_mean = _channel_tables(V7X_MXU_DIM)
    outs = []
    for x in (x_prompt, x_sample):
        l1, l2 = _seq_dft_tables(x.shape[1], SEQ_DFT_N1)
        outs.append(_trunk(x, params, (l1, l2, cc, cs, head_mean)))
    return tuple(outs)
```

```python
import functools
import math

import jax
import jax.numpy as jnp
import numpy as np
from jax import lax
from jax.experimental import pallas as pl
from jax.experimental.pallas import tpu as pltpu

HEAD_DIM = 64
EPS = 1e-6
F32 = jnp.float32
BF16 = jnp.bfloat16
U32 = jnp.uint32

V7X_LANES = 128
V7X_SUBLANES = 8
V7X_MXU_DIM = 256
VMEM_LIMIT_BYTES = 56 * 1024 * 1024

SEQ_DFT_N1 = 128
INPROJ_TILE = 1024
MIX_TILE = 1024
MIX_SUB_ROWS = 256
DFT_CHUNK = V7X_SUBLANES
DFT_MAX_CHUNKS_PER_STEP = 4
FF_CHUNK = 1024


def _rms(x, g):
    ms = jnp.mean(x * x, axis=-1, keepdims=True)
    return x * lax.rsqrt(ms + EPS) * g


def _const_spec(block_shape, index):
    return pl.BlockSpec(block_shape, lambda *_: index, pipeline_mode=pl.Buffered(1))


def _params():
    return pltpu.CompilerParams(dimension_semantics=("arbitrary", "arbitrary"),
                                vmem_limit_bytes=VMEM_LIMIT_BYTES)


def _lane_tile(x, t):
    return x[:, t * V7X_LANES:(t + 1) * V7X_LANES]


def _inproj_body(x_ref, g_ref, w_ref, gb_ref, z_ref, u_ref, *, dc, n2):
    h = _rms(x_ref[...], g_ref[...]).astype(BF16)
    p = jnp.dot(h, w_ref[...], preferred_element_type=F32)
    gb_ref[...] = p[:, :dc]
    z_ref[...] = p[:, dc:2 * dc] * p[:, 2 * dc:3 * dc]
    bits = pltpu.bitcast(p[:, 3 * dc:].astype(BF16).astype(F32), U32)
    n_chunks, half, rows, _ = u_ref.shape
    c = DFT_CHUNK
    for t in range(half):
        word = _lane_tile(bits, t) | (_lane_tile(bits, t + half) >> 16)
        word = word.reshape(rows // c, n_chunks, c, V7X_LANES)
        for j in range(n_chunks):
            u_ref[j, t] = word[:, j].reshape(rows, V7X_LANES)


def _inproj(x, g, w, layer, n2):
    b, s, d = x.shape
    dc = w.shape[-1] // 4
    t = INPROJ_TILE
    n_lt = dc // V7X_LANES
    c = DFT_CHUNK
    n_chunks = n2 // c
    rows = (t // n2) * c
    tok = lambda width: pl.BlockSpec((None, t, width), lambda bi, i: (bi, i, 0))
    out = jax.ShapeDtypeStruct((b, s, dc), F32)
    return pl.pallas_call(
        functools.partial(_inproj_body, dc=dc, n2=n2),
        grid=(b, s // t),
        in_specs=[tok(d),
                  _const_spec((None, 1, d), (layer, 0, 0)),
                  _const_spec((None, d, 4 * dc), (layer, 0, 0))],
        out_specs=[tok(dc), tok(dc),
                   pl.BlockSpec((None, n_chunks, n_lt // 2, rows, V7X_LANES), lambda bi, i: (bi, 0, 0, i, 0))],
        out_shape=[out, out, jax.ShapeDtypeStruct((b, n_chunks, n_lt // 2, (s // n2) * c, V7X_LANES), U32)],
        compiler_params=_params(),
        name=f"inproj_s{s}",
    )(x, g, w)


def _seq_dft_body(u_ref, l1_ref, l2_ref, cc_ref, cs_ref, y_ref, g_scr, *, n1, n2, c, p):
    j = pl.program_id(1)
    steps1 = n2 // (c * p)
    half = u_ref.shape[1]
    n_lt = 2 * half

    @pl.when(j < steps1)
    def _():
        for q in range(p):
            for s in range(c):
                words = [u_ref[q, t, pl.ds(s, n1, stride=c), :] for t in range(half)]
                slab = jnp.concatenate([pltpu.bitcast(w & jnp.uint32(0xFFFF0000), F32) for w in words]
                                       + [pltpu.bitcast(w << 16, F32) for w in words], axis=1)
                o = jnp.dot(l1_ref[q * c + s], slab.astype(BF16), preferred_element_type=F32)
                packed = pltpu.bitcast(o.astype(BF16), U32)
                for t in range(n_lt):
                    g_scr[t, j * p + q, pl.ds(s, n1, stride=c), :] = _lane_tile(packed, t)

    @pl.when(j >= steps1)
    def _():
        k0 = (j - steps1) * (p * c)
        vr, vi = [], []
        for k in range(p * c):
            rows = pl.ds((k0 + k) * c, c)
            w = jnp.concatenate([g_scr[t, :, rows, :].reshape(n2, V7X_LANES) for t in range(n_lt)], axis=1)
            v = jnp.dot(l2_ref[...], pltpu.bitcast(w, BF16), preferred_element_type=F32)
            vr.append(v[:n2].astype(BF16))
            vi.append(v[n2:].astype(BF16))
        vr = jnp.concatenate(vr, axis=0)
        vi = jnp.concatenate(vi, axis=0)
        group = cc_ref.shape[0]
        tiles = group // V7X_LANES
        for g in range(n_lt // tiles):
            cols = slice(g * group, (g + 1) * group)
            y = (jnp.dot(vr[:, cols], cc_ref[...], preferred_element_type=F32)
                 + jnp.dot(vi[:, cols], cs_ref[...], preferred_element_type=F32))
            for q in range(p):
                for k in range(c):
                    yk = y[(q * c + k) * n2:(q * c + k + 1) * n2]
                    for t in range(tiles):
                        y_ref[q, g * tiles + t, pl.ds(k, n2, stride=c), :] = _lane_tile(yk, t)


def _dft_chunks_per_step(n1, n2, width):
    c = DFT_CHUNK
    g_bytes = n1 * n2 * width * 4
    for p in range(DFT_MAX_CHUNKS_PER_STEP, 1, -1):
        io_bytes = 2 * p * c * ((n1 // 2 + n2) * width * 4 + 2 * n1 * n1 * 2)
        temp_bytes = p * c * n2 * width * (2 + 2 + 4) + 2 * n1 * width * 4 * 2
        if g_bytes + io_bytes + temp_bytes <= VMEM_LIMIT_BYTES and n2 % (p * c) == 0 and n1 % (p * c) == 0:
            return p
    return 1


def _seq_dft(u, l1, l2, cc, cs):
    b, n_chunks, half, _, _ = u.shape
    n_lt = 2 * half
    n2, _, n1 = l1.shape
    c = DFT_CHUNK
    p = _dft_chunks_per_step(n1, n2, n_lt * V7X_LANES)
    steps1 = n_chunks // p
    steps2 = n1 // (c * p)
    return pl.pallas_call(
        functools.partial(_seq_dft_body, n1=n1, n2=n2, c=c, p=p),
        grid=(b, steps1 + steps2),
        in_specs=[pl.BlockSpec((None, p, half, n1 * c, V7X_LANES),
                               lambda bi, j: (bi, jnp.minimum(j, steps1 - 1), 0, 0, 0)),
                  pl.BlockSpec((p * c, 2 * n1, n1), lambda bi, j: (jnp.minimum(j, steps1 - 1), 0, 0)),
                  _const_spec(l2.shape, (0, 0)),
                  _const_spec(cc.shape, (0, 0)),
                  _const_spec(cs.shape, (0, 0))],
        out_specs=pl.BlockSpec((None, p, n_lt, n2 * c, V7X_LANES),
                               lambda bi, j: (bi, jnp.maximum(j - steps1, 0), 0, 0, 0)),
        out_shape=jax.ShapeDtypeStruct((b, n1 // c, n_lt, n2 * c, V7X_LANES), F32),
        scratch_shapes=[pltpu.VMEM((n_lt, n_chunks, n1 * c, V7X_LANES), U32)],
        compiler_params=_params(),
        name=f"seq_dft_s{n1 * n2}",
    )(u, l1, l2, cc, cs)


def _mix_mlp_body(x_ref, gb_ref, z_ref, zp_ref, zn_ref, yf_ref, cw_ref, gco_ref, gfo_ref,
                  hm_ref, wo_ref, gpost_ref, gpre2_ref, wu_ref, wd_ref, gpost2_ref, o_ref,
                  *, n_tiles, sub_rows, ff_chunk):
    i = pl.program_id(1)
    r = sub_rows
    n_sub = z_ref.shape[0] // r
    n_ff = wu_ref.shape[-1] // ff_chunk
    halo = V7X_SUBLANES

    def head_norm(y, g):
        sq = (y * y).astype(BF16)
        group = hm_ref.shape[0]
        ms = jnp.concatenate(
            [jnp.dot(sq[:, q * group:(q + 1) * group], hm_ref[...], preferred_element_type=F32)
             for q in range(y.shape[1] // group)], axis=1)
        return y * lax.rsqrt(ms + EPS) * g

    def mix_inputs(s):
        rows = slice(s * r, (s + 1) * r)
        z = z_ref[rows, :]
        row = lax.broadcasted_iota(jnp.int32, z.shape, 0)
        if s == 0:
            z_before = jnp.where(i > 0, zp_ref[halo - 1:halo, :], 0.0)
        else:
            z_before = z_ref[s * r - 1:s * r, :]
        if s == n_sub - 1:
            z_after = jnp.where(i < n_tiles - 1, zn_ref[0:1, :], 0.0)
        else:
            z_after = z_ref[(s + 1) * r:(s + 1) * r + 1, :]
        z_m1 = jnp.where(row == 0, z_before, pltpu.roll(z, 1, 0))
        z_p1 = jnp.where(row == r - 1, z_after, pltpu.roll(z, r - 1, 0))
        cw = cw_ref[...]
        conv = z_m1 * cw[0:1] + z * cw[1:2] + z_p1 * cw[2:3]
        yc = gb_ref[rows, :] * conv
        n_kchunks, n_lt = yf_ref.shape[:2]
        c = DFT_CHUNK
        k2_per_sub = r // (n_kchunks * c)
        yf = jnp.concatenate(
            [jnp.concatenate([yf_ref[kc, t, k2 * c:(k2 + 1) * c, :]
                              for k2 in range(s * k2_per_sub, (s + 1) * k2_per_sub)
                              for kc in range(n_kchunks)], axis=0)
             for t in range(n_lt)], axis=1)
        merged = jnp.concatenate([head_norm(yc, gco_ref[...]), head_norm(yf, gfo_ref[...])], axis=-1)
        return merged.astype(BF16)

    def mix_out(s, merged):
        m = jnp.dot(merged, wo_ref[...], preferred_element_type=F32)
        x1 = x_ref[s * r:(s + 1) * r, :] + _rms(m, gpost_ref[...])
        return x1, _rms(x1, gpre2_ref[...]).astype(BF16)

    def mlp_chunk(h2, c):
        cols = slice(c * ff_chunk, (c + 1) * ff_chunk)
        a = jnp.maximum(jnp.dot(h2, wu_ref[:, cols], preferred_element_type=F32), 0.0)
        return jnp.dot((a * a).astype(BF16), wd_ref[cols, :], preferred_element_type=F32)

    x1, h2 = mix_out(0, mix_inputs(0))
    for s in range(n_sub):
        f = mlp_chunk(h2, 0)
        if s + 1 < n_sub:
            merged = mix_inputs(s + 1)
        f = f + mlp_chunk(h2, 1)
        if s + 1 < n_sub:
            nxt = mix_out(s + 1, merged)
        for c in range(2, n_ff):
            f = f + mlp_chunk(h2, c)
        o_ref[s * r:(s + 1) * r, :] = x1 + _rms(f, gpost2_ref[...])
        if s + 1 < n_sub:
            x1, h2 = nxt


def _mix_mlp(x, gb, z, yf, conv_w, g_co, g_fo, head_mean, w_out, g_post, g_pre2, w_up, w_down,
             g_post2, layer):
    b, s, d = x.shape
    dc = gb.shape[-1]
    d_ff = w_up.shape[-1]
    t = MIX_TILE
    n_tiles = s // t
    halo = V7X_SUBLANES
    hpt = t // halo
    n_halo = s // halo
    n_kchunks, n_lt = yf.shape[1:3]
    yf_rows = t // n_kchunks
    tok = lambda width: pl.BlockSpec((None, t, width), lambda bi, i: (bi, i, 0))
    vec = lambda width: _const_spec((None, 1, width), (layer, 0, 0))
    return pl.pallas_call(
        functools.partial(_mix_mlp_body, n_tiles=n_tiles, sub_rows=MIX_SUB_ROWS, ff_chunk=FF_CHUNK),
        grid=(b, n_tiles),
        in_specs=[tok(d), tok(dc), tok(dc),
                  pl.BlockSpec((None, halo, dc), lambda bi, i: (bi, jnp.maximum(i * hpt - 1, 0), 0)),
                  pl.BlockSpec((None, halo, dc), lambda bi, i: (bi, jnp.minimum((i + 1) * hpt, n_halo - 1), 0)),
                  pl.BlockSpec((None, n_kchunks, n_lt, yf_rows, V7X_LANES), lambda bi, i: (bi, 0, 0, i, 0)),
                  _const_spec((None, 3, dc), (layer, 0, 0)),
                  vec(dc), vec(dc),
                  _const_spec(head_mean.shape, (0, 0)),
                  _const_spec((None, d, d), (layer, 0, 0)),
                  vec(d), vec(d),
                  _const_spec((None, d, d_ff), (layer, 0, 0)),
                  _const_spec((None, d_ff, d), (layer, 0, 0)),
                  vec(d)],
        out_specs=tok(d),
        out_shape=jax.ShapeDtypeStruct((b, s, d), F32),
        compiler_params=_params(),
        name=f"mix_mlp_s{s}",
    )(x, gb, z, z, z, yf, conv_w, g_co, g_fo, head_mean, w_out, g_post, g_pre2, w_up, w_down, g_post2)


def _cos_sin(rows, cols, period):
    m = np.outer(np.arange(rows), np.arange(cols)) % period
    theta = m * (2.0 * np.pi / period)
    return np.cos(theta), np.sin(theta)


def _seq_dft_tables(s, n1):
    n2 = s // n1
    ca, sa = (jnp.asarray(t * (1.0 / math.sqrt(n1)), F32) for t in _cos_sin(n1, n1, n1))
    cb, sb = (jnp.asarray(t, F32)[:, :, None] for t in _cos_sin(n2, n1, s))
    re = ca[None] * cb - sa[None] * sb
    im = -(sa[None] * cb + ca[None] * sb)
    l1 = jnp.stack([re, im], axis=2).astype(BF16).reshape(n2, 2 * n1, n1)
    c2, s2 = _cos_sin(n2, n2, n2)
    l2 = np.stack([np.stack([c2, s2], axis=-1), np.stack([-s2, c2], axis=-1)], axis=0)
    l2 = jnp.asarray(l2.reshape(2 * n2, 2 * n2) * (1.0 / math.sqrt(n2)), BF16)
    return l1, l2


def _channel_tables(group):
    same = np.equal.outer(np.arange(group) // HEAD_DIM, np.arange(group) // HEAD_DIM)
    c, sn = _cos_sin(group, group, HEAD_DIM)
    scale = 1.0 / math.sqrt(HEAD_DIM)
    cc = jnp.asarray(np.where(same, c * scale, 0.0), BF16)
    cs = jnp.asarray(np.where(same, sn * scale, 0.0), BF16)
    head_mean = jnp.asarray(np.where(same, 1.0 / HEAD_DIM, 0.0), BF16)
    return cc, cs, head_mean


def _trunk(x, params, tables):
    (g_mix_pre, w_in, conv_w, g_conv_out, g_fourier_out, w_out, g_mix_post, g_mlp_pre, w_up,
     w_down, g_mlp_post) = params
    l1, l2, cc, cs, head_mean = tables
    depth = w_in.shape[0]
    for layer in range(depth):
        gb, z, u = _inproj(x, g_mix_pre, w_in, layer, l1.shape[0])
        yf = _seq_dft(u, l1, l2, cc, cs)
        x = _mix_mlp(x, gb, z, yf, conv_w, g_conv_out, g_fourier_out, head_mean, w_out,
                     g_mix_post, g_mlp_pre, w_up, w_down, g_mlp_post, layer)
    return x


def kernel(x_prompt, x_sample, g_mix_pre, w_in, conv_w, g_conv_out, g_fourier_out, w_out,
           g_mix_post, g_mlp_pre, w_up, w_down, g_mlp_post):
    depth = w_in.shape[0]
    vec3 = lambda g: g.reshape(depth, 1, g.shape[-1])
    params = (vec3(g_mix_pre), w_in.astype(BF16), conv_w, vec3(g_conv_out), vec3(g_fourier_out),
              w_out.astype(BF16), vec3(g_mix_post), vec3(g_mlp_pre), w_up.astype(BF16),
              w_down.astype(BF16), vec3(g_mlp_post))
    cc, cs, head_mean = _channel_tables(V7X_MXU_DIM)
    outs = []
    for x in (x_prompt, x_sample):
        l1, l2 = _seq_dft_tables(x.shape[1], SEQ_DFT_N1)
        outs.append(_trunk(x, params, (l1, l2, cc, cs, head_mean)))
    return tuple(outs)
```

```python
import functools
import math

import jax
import jax.numpy as jnp
import numpy as np
from jax import lax
from jax.experimental import pallas as pl
from jax.experimental.pallas import tpu as pltpu

HEAD_DIM = 64
EPS = 1e-6
F32 = jnp.float32
BF16 = jnp.bfloat16
U32 = jnp.uint32

V7X_LANES = 128
V7X_SUBLANES = 8
V7X_MXU_DIM = 256
VMEM_LIMIT_BYTES = 56 * 1024 * 1024

SEQ_DFT_N1 = 128
INPROJ_TILE = 1024
INPROJ_SUB_ROWS = 256
MIX_TILE = 1024
MIX_SUB_ROWS = 256
DFT_CHUNK = V7X_SUBLANES
DFT_MAX_CHUNKS_PER_STEP = 4
FF_CHUNK = 1024


def _rms(x, g):
    ms = jnp.mean(x * x, axis=-1, keepdims=True)
    return x * lax.rsqrt(ms + EPS) * g


def _const_spec(block_shape, index):
    return pl.BlockSpec(block_shape, lambda *_: index, pipeline_mode=pl.Buffered(1))


def _params():
    return pltpu.CompilerParams(dimension_semantics=("arbitrary", "arbitrary"),
                                vmem_limit_bytes=VMEM_LIMIT_BYTES)


def _lane_tile(x, t):
    return x[:, t * V7X_LANES:(t + 1) * V7X_LANES]


def _inproj_body(x_ref, g_ref, w_ref, gb_ref, z_ref, u_ref, *, dc, n2):
    r = INPROJ_SUB_ROWS
    n_chunks, half, _, _ = u_ref.shape
    c = DFT_CHUNK
    rows = (r // n2) * c

    def norm(s):
        return _rms(x_ref[s * r:(s + 1) * r, :], g_ref[...]).astype(BF16)

    def emit(s, p):
        gb_ref[s * r:(s + 1) * r, :] = p[:, :dc]
        z_ref[s * r:(s + 1) * r, :] = p[:, dc:2 * dc] * p[:, 2 * dc:3 * dc]
        bits = pltpu.bitcast(p[:, 3 * dc:].astype(BF16).astype(F32), U32)
        for t in range(half):
            word = _lane_tile(bits, t) | (_lane_tile(bits, t + half) >> 16)
            word = word.reshape(rows // c, n_chunks, c, V7X_LANES)
            for j in range(n_chunks):
                u_ref[j, t, s * rows:(s + 1) * rows, :] = word[:, j].reshape(rows, V7X_LANES)

    h = norm(0)
    for s in range(x_ref.shape[0] // r):
        p = jnp.dot(h, w_ref[...], preferred_element_type=F32)
        if (s + 1) * r < x_ref.shape[0]:
            h = norm(s + 1)
        emit(s, p)


def _inproj(x, g, w, layer, n2):
    b, s, d = x.shape
    dc = w.shape[-1] // 4
    t = INPROJ_TILE
    n_lt = dc // V7X_LANES
    c = DFT_CHUNK
    n_chunks = n2 // c
    rows = (t // n2) * c
    tok = lambda width: pl.BlockSpec((None, t, width), lambda bi, i: (bi, i, 0))
    out = jax.ShapeDtypeStruct((b, s, dc), F32)
    return pl.pallas_call(
        functools.partial(_inproj_body, dc=dc, n2=n2),
        grid=(b, s // t),
        in_specs=[tok(d),
                  _const_spec((None, 1, d), (layer, 0, 0)),
                  _const_spec((None, d, 4 * dc), (layer, 0, 0))],
        out_specs=[tok(dc), tok(dc),
                   pl.BlockSpec((None, n_chunks, n_lt // 2, rows, V7X_LANES), lambda bi, i: (bi, 0, 0, i, 0))],
        out_shape=[out, out, jax.ShapeDtypeStruct((b, n_chunks, n_lt // 2, (s // n2) * c, V7X_LANES), U32)],
        compiler_params=_params(),
        name=f"inproj_s{s}",
    )(x, g, w)


def _seq_dft_body(u_ref, l1_ref, l2_ref, cc_ref, cs_ref, y_ref, g_scr, *, n1, n2, c, p):
    j = pl.program_id(1)
    steps1 = n2 // (c * p)
    half = u_ref.shape[1]
    n_lt = 2 * half

    @pl.when(j < steps1)
    def _():
        for q in range(p):
            for s in range(c):
                words = [u_ref[q, t, pl.ds(s, n1, stride=c), :] for t in range(half)]
                slab = jnp.concatenate([pltpu.bitcast(w & jnp.uint32(0xFFFF0000), F32) for w in words]
                                       + [pltpu.bitcast(w << 16, F32) for w in words], axis=1)
                o = jnp.dot(l1_ref[q * c + s], slab.astype(BF16), preferred_element_type=F32)
                packed = pltpu.bitcast(o.astype(BF16), U32)
                for t in range(n_lt):
                    g_scr[t, j * p + q, pl.ds(s, n1, stride=c), :] = _lane_tile(packed, t)

    @pl.when(j >= steps1)
    def _():
        k0 = (j - steps1) * (p * c)
        vr, vi = [], []
        for k in range(p * c):
            rows = pl.ds((k0 + k) * c, c)
            w = jnp.concatenate([g_scr[t, :, rows, :].reshape(n2, V7X_LANES) for t in range(n_lt)], axis=1)
            v = jnp.dot(l2_ref[...], pltpu.bitcast(w, BF16), preferred_element_type=F32)
            vr.append(v[:n2].astype(BF16))
            vi.append(v[n2:].astype(BF16))
        vr = jnp.concatenate(vr, axis=0)
        vi = jnp.concatenate(vi, axis=0)
        group = cc_ref.shape[0]
        tiles = group // V7X_LANES
        for g in range(n_lt // tiles):
            cols = slice(g * group, (g + 1) * group)
            y = (jnp.dot(vr[:, cols], cc_ref[...], preferred_element_type=F32)
                 + jnp.dot(vi[:, cols], cs_ref[...], preferred_element_type=F32))
            for q in range(p):
                for k in range(c):
                    yk = y[(q * c + k) * n2:(q * c + k + 1) * n2]
                    for t in range(tiles):
                        y_ref[q, g * tiles + t, pl.ds(k, n2, stride=c), :] = _lane_tile(yk, t)


def _dft_chunks_per_step(n1, n2, width):
    c = DFT_CHUNK
    g_bytes = n1 * n2 * width * 4
    for p in range(DFT_MAX_CHUNKS_PER_STEP, 1, -1):
        io_bytes = 2 * p * c * ((n1 // 2 + n2) * width * 4 + 2 * n1 * n1 * 2)
        temp_bytes = p * c * n2 * width * (2 + 2 + 4) + 2 * n1 * width * 4 * 2
        if g_bytes + io_bytes + temp_bytes <= VMEM_LIMIT_BYTES and n2 % (p * c) == 0 and n1 % (p * c) == 0:
            return p
    return 1


def _seq_dft(u, l1, l2, cc, cs):
    b, n_chunks, half, _, _ = u.shape
    n_lt = 2 * half
    n2, _, n1 = l1.shape
    c = DFT_CHUNK
    p = _dft_chunks_per_step(n1, n2, n_lt * V7X_LANES)
    steps1 = n_chunks // p
    steps2 = n1 // (c * p)
    return pl.pallas_call(
        functools.partial(_seq_dft_body, n1=n1, n2=n2, c=c, p=p),
        grid=(b, steps1 + steps2),
        in_specs=[pl.BlockSpec((None, p, half, n1 * c, V7X_LANES),
                               lambda bi, j: (bi, jnp.minimum(j, steps1 - 1), 0, 0, 0)),
                  pl.BlockSpec((p * c, 2 * n1, n1), lambda bi, j: (jnp.minimum(j, steps1 - 1), 0, 0)),
                  _const_spec(l2.shape, (0, 0)),
                  _const_spec(cc.shape, (0, 0)),
                  _const_spec(cs.shape, (0, 0))],
        out_specs=pl.BlockSpec((None, p, n_lt, n2 * c, V7X_LANES),
                               lambda bi, j: (bi, jnp.maximum(j - steps1, 0), 0, 0, 0)),
        out_shape=jax.ShapeDtypeStruct((b, n1 // c, n_lt, n2 * c, V7X_LANES), F32),
        scratch_shapes=[pltpu.VMEM((n_lt, n_chunks, n1 * c, V7X_LANES), U32)],
        compiler_params=_params(),
        name=f"seq_dft_s{n1 * n2}",
    )(u, l1, l2, cc, cs)


def _mix_mlp_body(x_ref, gb_ref, z_ref, zp_ref, zn_ref, yf_ref, cw_ref, gco_ref, gfo_ref,
                  hm_ref, wo_ref, gpost_ref, gpre2_ref, wu_ref, wd_ref, gpost2_ref, o_ref,
                  *, n_tiles, sub_rows, ff_chunk):
    i = pl.program_id(1)
    r = sub_rows
    n_sub = z_ref.shape[0] // r
    n_ff = wu_ref.shape[-1] // ff_chunk
    halo = V7X_SUBLANES

    def head_norm(y, g):
        sq = (y * y).astype(BF16)
        group = hm_ref.shape[0]
        ms = jnp.concatenate(
            [jnp.dot(sq[:, q * group:(q + 1) * group], hm_ref[...], preferred_element_type=F32)
             for q in range(y.shape[1] // group)], axis=1)
        return y * lax.rsqrt(ms + EPS) * g

    def mix_inputs(s):
        rows = slice(s * r, (s + 1) * r)
        z = z_ref[rows, :]
        row = lax.broadcasted_iota(jnp.int32, z.shape, 0)
        if s == 0:
            z_before = jnp.where(i > 0, zp_ref[halo - 1:halo, :], 0.0)
        else:
            z_before = z_ref[s * r - 1:s * r, :]
        if s == n_sub - 1:
            z_after = jnp.where(i < n_tiles - 1, zn_ref[0:1, :], 0.0)
        else:
            z_after = z_ref[(s + 1) * r:(s + 1) * r + 1, :]
        z_m1 = jnp.where(row == 0, z_before, pltpu.roll(z, 1, 0))
        z_p1 = jnp.where(row == r - 1, z_after, pltpu.roll(z, r - 1, 0))
        cw = cw_ref[...]
        conv = z_m1 * cw[0:1] + z * cw[1:2] + z_p1 * cw[2:3]
        yc = gb_ref[rows, :] * conv
        n_kchunks, n_lt = yf_ref.shape[:2]
        c = DFT_CHUNK
        k2_per_sub = r // (n_kchunks * c)
        yf = jnp.concatenate(
            [jnp.concatenate([yf_ref[kc, t, k2 * c:(k2 + 1) * c, :]
                              for k2 in range(s * k2_per_sub, (s + 1) * k2_per_sub)
                              for kc in range(n_kchunks)], axis=0)
             for t in range(n_lt)], axis=1)
        merged = jnp.concatenate([head_norm(yc, gco_ref[...]), head_norm(yf, gfo_ref[...])], axis=-1)
        return merged.astype(BF16)

    def mix_out(s, merged):
        m = jnp.dot(merged, wo_ref[...], preferred_element_type=F32)
        x1 = x_ref[s * r:(s + 1) * r, :] + _rms(m, gpost_ref[...])
        return x1, _rms(x1, gpre2_ref[...]).astype(BF16)

    def mlp_chunk(h2, c):
        cols = slice(c * ff_chunk, (c + 1) * ff_chunk)
        a = jnp.maximum(jnp.dot(h2, wu_ref[:, cols], preferred_element_type=F32), 0.0)
        return jnp.dot((a * a).astype(BF16), wd_ref[cols, :], preferred_element_type=F32)

    x1, h2 = mix_out(0, mix_inputs(0))
    for s in range(n_sub):
        f = mlp_chunk(h2, 0)
        if s + 1 < n_sub:
            merged = mix_inputs(s + 1)
        f = f + mlp_chunk(h2, 1)
        if s + 1 < n_sub:
            nxt = mix_out(s + 1, merged)
        for c in range(2, n_ff):
            f = f + mlp_chunk(h2, c)
        o_ref[s * r:(s + 1) * r, :] = x1 + _rms(f, gpost2_ref[...])
        if s + 1 < n_sub:
            x1, h2 = nxt


def _mix_mlp(x, gb, z, yf, conv_w, g_co, g_fo, head_mean, w_out, g_post, g_pre2, w_up, w_down,
             g_post2, layer):
    b, s, d = x.shape
    dc = gb.shape[-1]
    d_ff = w_up.shape[-1]
    t = MIX_TILE
    n_tiles = s // t
    halo = V7X_SUBLANES
    hpt = t // halo
    n_halo = s // halo
    n_kchunks, n_lt = yf.shape[1:3]
    yf_rows = t // n_kchunks
    tok = lambda width: pl.BlockSpec((None, t, width), lambda bi, i: (bi, i, 0))
    vec = lambda width: _const_spec((None, 1, width), (layer, 0, 0))
    return pl.pallas_call(
        functools.partial(_mix_mlp_body, n_tiles=n_tiles, sub_rows=MIX_SUB_ROWS, ff_chunk=FF_CHUNK),
        grid=(b, n_tiles),
        in_specs=[tok(d), tok(dc), tok(dc),
                  pl.BlockSpec((None, halo, dc), lambda bi, i: (bi, jnp.maximum(i * hpt - 1, 0), 0)),
                  pl.BlockSpec((None, halo, dc), lambda bi, i: (bi, jnp.minimum((i + 1) * hpt, n_halo - 1), 0)),
                  pl.BlockSpec((None, n_kchunks, n_lt, yf_rows, V7X_LANES), lambda bi, i: (bi, 0, 0, i, 0)),
                  _const_spec((None, 3, dc), (layer, 0, 0)),
                  vec(dc), vec(dc),
                  _const_spec(head_mean.shape, (0, 0)),
                  _const_spec((None, d, d), (layer, 0, 0)),
                  vec(d), vec(d),
                  _const_spec((None, d, d_ff), (layer, 0, 0)),
                  _const_spec((None, d_ff, d), (layer, 0, 0)),
                  vec(d)],
        out_specs=tok(d),
        out_shape=jax.ShapeDtypeStruct((b, s, d), F32),
        compiler_params=_params(),
        name=f"mix_mlp_s{s}",
    )(x, gb, z, z, z, yf, conv_w, g_co, g_fo, head_mean, w_out, g_post, g_pre2, w_up, w_down, g_post2)


def _cos_sin(rows, cols, period):
    m = np.outer(np.arange(rows), np.arange(cols)) % period
    theta = m * (2.0 * np.pi / period)
    return np.cos(theta), np.sin(theta)


def _seq_dft_tables(s, n1):
    n2 = s // n1
    ca, sa = (t * (1.0 / math.sqrt(n1)) for t in _cos_sin(n1, n1, n1))
    cb, sb = _cos_sin(n2, n1, s)
    interleave = lambda re, im: jnp.asarray(np.stack([re, im], axis=1).reshape(2 * n1, n1), F32)
    a_cos, a_sin = interleave(ca, -sa), interleave(-sa, -ca)
    b_cos, b_sin = (jnp.asarray(np.repeat(t, 2, axis=1), F32)[:, :, None] for t in (cb, sb))
    l1 = (a_cos[None] * b_cos + a_sin[None] * b_sin).astype(BF16)
    c2, s2 = _cos_sin(n2, n2, n2)
    l2 = np.stack([np.stack([c2, s2], axis=-1), np.stack([-s2, c2], axis=-1)], axis=0)
    l2 = jnp.asarray(l2.reshape(2 * n2, 2 * n2) * (1.0 / math.sqrt(n2)), BF16)
    return l1, l2


def _channel_tables(group):
    same = np.equal.outer(np.arange(group) // HEAD_DIM, np.arange(group) // HEAD_DIM)
    c, sn = _cos_sin(group, group, HEAD_DIM)
    scale = 1.0 / math.sqrt(HEAD_DIM)
    cc = jnp.asarray(np.where(same, c * scale, 0.0), BF16)
    cs = jnp.asarray(np.where(same, sn * scale, 0.0), BF16)
    head_mean = jnp.asarray(np.where(same, 1.0 / HEAD_DIM, 0.0), BF16)
    return cc, cs, head_mean


def _trunk(x, params, tables):
    (g_mix_pre, w_in, conv_w, g_conv_out, g_fourier_out, w_out, g_mix_post, g_mlp_pre, w_up,
     w_down, g_mlp_post) = params
    l1, l2, cc, cs, head_mean = tables
    depth = w_in.shape[0]
    for layer in range(depth):
        gb, z, u = _inproj(x, g_mix_pre, w_in, layer, l1.shape[0])
        yf = _seq_dft(u, l1, l2, cc, cs)
        x = _mix_mlp(x, gb, z, yf, conv_w, g_conv_out, g_fourier_out, head_mean, w_out,
                     g_mix_post, g_mlp_pre, w_up, w_down, g_mlp_post, layer)
    return x


def kernel(x_prompt, x_sample, g_mix_pre, w_in, conv_w, g_conv_out, g_fourier_out, w_out,
           g_mix_post, g_mlp_pre, w_up, w_down, g_mlp_post):
    depth = w_in.shape[0]
    vec3 = lambda g: g.reshape(depth, 1, g.shape[-1])
    params = (vec3(g_mix_pre), w_in.astype(BF16), conv_w, vec3(g_conv_out), vec3(g_fourier_out),
              w_out.astype(BF16), vec3(g_mix_post), vec3(g_mlp_pre), w_up.astype(BF16),
              w_down.astype(BF16), vec3(g_mlp_post))
    cc, cs, head_mean = _channel_tables(V7X_MXU_DIM)
    outs = []
    for x in (x_prompt, x_sample):
        l1, l2 = _seq_dft_tables(x.shape[1], SEQ_DFT_N1)
        outs.append(_trunk(x, params, (l1, l2, cc, cs, head_mean)))
    return tuple(outs)
```

```python
import functools
import math

import jax
import jax.numpy as jnp
import numpy as np
from jax import lax
from jax.experimental import pallas as pl
from jax.experimental.pallas import tpu as pltpu

HEAD_DIM = 64
EPS = 1e-6
F32 = jnp.float32
BF16 = jnp.bfloat16
U32 = jnp.uint32

V7X_LANES = 128
V7X_SUBLANES = 8
V7X_MXU_DIM = 256
VMEM_LIMIT_BYTES = 56 * 1024 * 1024

SEQ_DFT_N1 = 128
INPROJ_TILE = 1024
INPROJ_SUB_ROWS = 256
MIX_TILE = 1024
MIX_SUB_ROWS = 256
DFT_CHUNK = V7X_SUBLANES
DFT_MAX_CHUNKS_PER_STEP = 4
FF_CHUNK = 1024


def _rms(x, g):
    ms = jnp.mean(x * x, axis=-1, keepdims=True)
    return x * lax.rsqrt(ms + EPS) * g


def _const_spec(block_shape, index):
    return pl.BlockSpec(block_shape, lambda *_: index, pipeline_mode=pl.Buffered(1))


def _params():
    return pltpu.CompilerParams(dimension_semantics=("arbitrary", "arbitrary"),
                                vmem_limit_bytes=VMEM_LIMIT_BYTES)


def _lane_tile(x, t):
    return x[:, t * V7X_LANES:(t + 1) * V7X_LANES]


def _inproj_body(x_ref, g_ref, w_ref, gb_ref, z_ref, u_ref, *, dc, n2):
    r = INPROJ_SUB_ROWS
    n_chunks, half, _, _ = u_ref.shape
    c = DFT_CHUNK
    rows = (r // n2) * c

    def norm(s):
        return _rms(x_ref[s * r:(s + 1) * r, :], g_ref[...]).astype(BF16)

    def emit(s, p):
        gb_ref[s * r:(s + 1) * r, :] = p[:, :dc]
        z_ref[s * r:(s + 1) * r, :] = p[:, dc:2 * dc] * p[:, 2 * dc:3 * dc]
        bits = pltpu.bitcast(p[:, 3 * dc:].astype(BF16).astype(F32), U32)
        for t in range(half):
            word = _lane_tile(bits, t) | (_lane_tile(bits, t + half) >> 16)
            word = word.reshape(rows // c, n_chunks, c, V7X_LANES)
            for j in range(n_chunks):
                u_ref[j, t, s * rows:(s + 1) * rows, :] = word[:, j].reshape(rows, V7X_LANES)

    h = norm(0)
    for s in range(x_ref.shape[0] // r):
        p = jnp.dot(h, w_ref[...], preferred_element_type=F32)
        if (s + 1) * r < x_ref.shape[0]:
            h = norm(s + 1)
        emit(s, p)


def _inproj(x, g, w, layer, n2):
    b, s, d = x.shape
    dc = w.shape[-1] // 4
    t = INPROJ_TILE
    n_lt = dc // V7X_LANES
    c = DFT_CHUNK
    n_chunks = n2 // c
    rows = (t // n2) * c
    tok = lambda width: pl.BlockSpec((None, t, width), lambda bi, i: (bi, i, 0))
    out = jax.ShapeDtypeStruct((b, s, dc), F32)
    return pl.pallas_call(
        functools.partial(_inproj_body, dc=dc, n2=n2),
        grid=(b, s // t),
        in_specs=[tok(d),
                  _const_spec((None, 1, d), (layer, 0, 0)),
                  _const_spec((None, d, 4 * dc), (layer, 0, 0))],
        out_specs=[tok(dc), tok(dc),
                   pl.BlockSpec((None, n_chunks, n_lt // 2, rows, V7X_LANES), lambda bi, i: (bi, 0, 0, i, 0))],
        out_shape=[out, out, jax.ShapeDtypeStruct((b, n_chunks, n_lt // 2, (s // n2) * c, V7X_LANES), U32)],
        compiler_params=_params(),
        name=f"inproj_s{s}",
    )(x, g, w)


def _seq_dft_body(u_ref, l1_ref, l2_ref, cc_ref, cs_ref, y_ref, g_scr, *, n1, n2, c, p):
    j = pl.program_id(1)
    steps1 = n2 // (c * p)
    half = u_ref.shape[1]
    n_lt = 2 * half
    sub = V7X_SUBLANES

    @pl.when(j < steps1)
    def _():
        for q in range(p):
            for s in range(c):
                words = [u_ref[q, t, pl.ds(s, n1, stride=c), :] for t in range(half)]
                slab = jnp.concatenate([pltpu.bitcast(w & jnp.uint32(0xFFFF0000), F32) for w in words]
                                       + [pltpu.bitcast(w << 16, F32) for w in words], axis=1)
                o = jnp.dot(l1_ref[q * c + s], slab.astype(BF16), preferred_element_type=F32)
                packed = pltpu.bitcast(o.astype(BF16), U32)
                rows = pl.ds(pl.multiple_of(((j * p + q) * c + s) * sub, sub), sub)
                for t in range(n_lt):
                    tile = _lane_tile(packed, t)
                    for kg in range(n1 // sub):
                        g_scr[t, kg, rows, :] = tile[kg * sub:(kg + 1) * sub]

    @pl.when(j >= steps1)
    def _():
        k0 = (j - steps1) * (p * c)
        vr, vi = [], []
        for k in range(p * c):
            kg = k0 // sub + k // sub
            w = jnp.concatenate([g_scr[t, kg, pl.ds(k % sub, n2, stride=sub), :] for t in range(n_lt)], axis=1)
            v = jnp.dot(l2_ref[...], pltpu.bitcast(w, BF16), preferred_element_type=F32)
            vr.append(v[:n2].astype(BF16))
            vi.append(v[n2:].astype(BF16))
        vr = jnp.concatenate(vr, axis=0)
        vi = jnp.concatenate(vi, axis=0)
        group = cc_ref.shape[0]
        tiles = group // V7X_LANES
        for g in range(n_lt // tiles):
            cols = slice(g * group, (g + 1) * group)
            y = (jnp.dot(vr[:, cols], cc_ref[...], preferred_element_type=F32)
                 + jnp.dot(vi[:, cols], cs_ref[...], preferred_element_type=F32))
            for q in range(p):
                for k in range(c):
                    yk = y[(q * c + k) * n2:(q * c + k + 1) * n2]
                    for t in range(tiles):
                        y_ref[q, g * tiles + t, pl.ds(k, n2, stride=c), :] = _lane_tile(yk, t)


def _dft_chunks_per_step(n1, n2, width):
    c = DFT_CHUNK
    g_bytes = n1 * n2 * width * 4
    for p in range(DFT_MAX_CHUNKS_PER_STEP, 1, -1):
        io_bytes = 2 * p * c * ((n1 // 2 + n2) * width * 4 + 2 * n1 * n1 * 2)
        temp_bytes = p * c * n2 * width * (2 + 2 + 4) + 2 * n1 * width * 4 * 2
        if g_bytes + io_bytes + temp_bytes <= VMEM_LIMIT_BYTES and n2 % (p * c) == 0 and n1 % (p * c) == 0:
            return p
    return 1


def _seq_dft(u, l1, l2, cc, cs):
    b, n_chunks, half, _, _ = u.shape
    n_lt = 2 * half
    n2, _, n1 = l1.shape
    c = DFT_CHUNK
    p = _dft_chunks_per_step(n1, n2, n_lt * V7X_LANES)
    steps1 = n_chunks // p
    steps2 = n1 // (c * p)
    return pl.pallas_call(
        functools.partial(_seq_dft_body, n1=n1, n2=n2, c=c, p=p),
        grid=(b, steps1 + steps2),
        in_specs=[pl.BlockSpec((None, p, half, n1 * c, V7X_LANES),
                               lambda bi, j: (bi, jnp.minimum(j, steps1 - 1), 0, 0, 0)),
                  pl.BlockSpec((p * c, 2 * n1, n1), lambda bi, j: (jnp.minimum(j, steps1 - 1), 0, 0)),
                  _const_spec(l2.shape, (0, 0)),
                  _const_spec(cc.shape, (0, 0)),
                  _const_spec(cs.shape, (0, 0))],
        out_specs=pl.BlockSpec((None, p, n_lt, n2 * c, V7X_LANES),
                               lambda bi, j: (bi, jnp.maximum(j - steps1, 0), 0, 0, 0)),
        out_shape=jax.ShapeDtypeStruct((b, n1 // c, n_lt, n2 * c, V7X_LANES), F32),
        scratch_shapes=[pltpu.VMEM((n_lt, n1 // V7X_SUBLANES, n2 * V7X_SUBLANES, V7X_LANES), U32)],
        compiler_params=_params(),
        name=f"seq_dft_s{n1 * n2}",
    )(u, l1, l2, cc, cs)


def _mix_mlp_body(x_ref, gb_ref, z_ref, zp_ref, zn_ref, yf_ref, cw_ref, gco_ref, gfo_ref,
                  hm_ref, wo_ref, gpost_ref, gpre2_ref, wu_ref, wd_ref, gpost2_ref, o_ref,
                  *, n_tiles, sub_rows, ff_chunk):
    i = pl.program_id(1)
    r = sub_rows
    n_sub = z_ref.shape[0] // r
    n_ff = wu_ref.shape[-1] // ff_chunk
    halo = V7X_SUBLANES

    def head_norm(y, g):
        sq = (y * y).astype(BF16)
        group = hm_ref.shape[0]
        ms = jnp.concatenate(
            [jnp.dot(sq[:, q * group:(q + 1) * group], hm_ref[...], preferred_element_type=F32)
             for q in range(y.shape[1] // group)], axis=1)
        return y * lax.rsqrt(ms + EPS) * g

    def mix_inputs(s):
        rows = slice(s * r, (s + 1) * r)
        z = z_ref[rows, :]
        row = lax.broadcasted_iota(jnp.int32, z.shape, 0)
        if s == 0:
            z_before = jnp.where(i > 0, zp_ref[halo - 1:halo, :], 0.0)
        else:
            z_before = z_ref[s * r - 1:s * r, :]
        if s == n_sub - 1:
            z_after = jnp.where(i < n_tiles - 1, zn_ref[0:1, :], 0.0)
        else:
            z_after = z_ref[(s + 1) * r:(s + 1) * r + 1, :]
        z_m1 = jnp.where(row == 0, z_before, pltpu.roll(z, 1, 0))
        z_p1 = jnp.where(row == r - 1, z_after, pltpu.roll(z, r - 1, 0))
        cw = cw_ref[...]
        conv = z_m1 * cw[0:1] + z * cw[1:2] + z_p1 * cw[2:3]
        yc = gb_ref[rows, :] * conv
        n_kchunks, n_lt = yf_ref.shape[:2]
        c = DFT_CHUNK
        k2_per_sub = r // (n_kchunks * c)
        yf = jnp.concatenate(
            [jnp.concatenate([yf_ref[kc, t, k2 * c:(k2 + 1) * c, :]
                              for k2 in range(s * k2_per_sub, (s + 1) * k2_per_sub)
                              for kc in range(n_kchunks)], axis=0)
             for t in range(n_lt)], axis=1)
        merged = jnp.concatenate([head_norm(yc, gco_ref[...]), head_norm(yf, gfo_ref[...])], axis=-1)
        return merged.astype(BF16)

    def mix_out(s, merged):
        m = jnp.dot(merged, wo_ref[...], preferred_element_type=F32)
        x1 = x_ref[s * r:(s + 1) * r, :] + _rms(m, gpost_ref[...])
        return x1, _rms(x1, gpre2_ref[...]).astype(BF16)

    def mlp_chunk(h2, c):
        cols = slice(c * ff_chunk, (c + 1) * ff_chunk)
        a = jnp.maximum(jnp.dot(h2, wu_ref[:, cols], preferred_element_type=F32), 0.0)
        return jnp.dot((a * a).astype(BF16), wd_ref[cols, :], preferred_element_type=F32)

    x1, h2 = mix_out(0, mix_inputs(0))
    for s in range(n_sub):
        f = mlp_chunk(h2, 0)
        if s + 1 < n_sub:
            merged = mix_inputs(s + 1)
        f = f + mlp_chunk(h2, 1)
        if s + 1 < n_sub:
            nxt = mix_out(s + 1, merged)
        for c in range(2, n_ff):
            f = f + mlp_chunk(h2, c)
        o_ref[s * r:(s + 1) * r, :] = x1 + _rms(f, gpost2_ref[...])
        if s + 1 < n_sub:
            x1, h2 = nxt


def _mix_mlp(x, gb, z, yf, conv_w, g_co, g_fo, head_mean, w_out, g_post, g_pre2, w_up, w_down,
             g_post2, layer):
    b, s, d = x.shape
    dc = gb.shape[-1]
    d_ff = w_up.shape[-1]
    t = MIX_TILE
    n_tiles = s // t
    halo = V7X_SUBLANES
    hpt = t // halo
    n_halo = s // halo
    n_kchunks, n_lt = yf.shape[1:3]
    yf_rows = t // n_kchunks
    tok = lambda width: pl.BlockSpec((None, t, width), lambda bi, i: (bi, i, 0))
    vec = lambda width: _const_spec((None, 1, width), (layer, 0, 0))
    return pl.pallas_call(
        functools.partial(_mix_mlp_body, n_tiles=n_tiles, sub_rows=MIX_SUB_ROWS, ff_chunk=FF_CHUNK),
        grid=(b, n_tiles),
        in_specs=[tok(d), tok(dc), tok(dc),
                  pl.BlockSpec((None, halo, dc), lambda bi, i: (bi, jnp.maximum(i * hpt - 1, 0), 0)),
                  pl.BlockSpec((None, halo, dc), lambda bi, i: (bi, jnp.minimum((i + 1) * hpt, n_halo - 1), 0)),
                  pl.BlockSpec((None, n_kchunks, n_lt, yf_rows, V7X_LANES), lambda bi, i: (bi, 0, 0, i, 0)),
                  _const_spec((None, 3, dc), (layer, 0, 0)),
                  vec(dc), vec(dc),
                  _const_spec(head_mean.shape, (0, 0)),
                  _const_spec((None, d, d), (layer, 0, 0)),
                  vec(d), vec(d),
                  _const_spec((None, d, d_ff), (layer, 0, 0)),
                  _const_spec((None, d_ff, d), (layer, 0, 0)),
                  vec(d)],
        out_specs=tok(d),
        out_shape=jax.ShapeDtypeStruct((b, s, d), F32),
        compiler_params=_params(),
        name=f"mix_mlp_s{s}",
    )(x, gb, z, z, z, yf, conv_w, g_co, g_fo, head_mean, w_out, g_post, g_pre2, w_up, w_down, g_post2)


def _cos_sin(rows, cols, period):
    m = np.outer(np.arange(rows), np.arange(cols)) % period
    theta = m * (2.0 * np.pi / period)
    return np.cos(theta), np.sin(theta)


def _seq_dft_tables(s, n1):
    n2 = s // n1
    ca, sa = (t * (1.0 / math.sqrt(n1)) for t in _cos_sin(n1, n1, n1))
    cb, sb = _cos_sin(n2, n1, s)
    interleave = lambda re, im: jnp.asarray(np.stack([re, im], axis=1).reshape(2 * n1, n1), F32)
    a_cos, a_sin = interleave(ca, -sa), interleave(-sa, -ca)
    b_cos, b_sin = (jnp.asarray(np.repeat(t, 2, axis=1), F32)[:, :, None] for t in (cb, sb))
    l1 = (a_cos[None] * b_cos + a_sin[None] * b_sin).astype(BF16)
    c2, s2 = _cos_sin(n2, n2, n2)
    l2 = np.stack([np.stack([c2, s2], axis=-1), np.stack([-s2, c2], axis=-1)], axis=0)
    l2 = jnp.asarray(l2.reshape(2 * n2, 2 * n2) * (1.0 / math.sqrt(n2)), BF16)
    return l1, l2


def _channel_tables(group):
    same = np.equal.outer(np.arange(group) // HEAD_DIM, np.arange(group) // HEAD_DIM)
    c, sn = _cos_sin(group, group, HEAD_DIM)
    scale = 1.0 / math.sqrt(HEAD_DIM)
    cc = jnp.asarray(np.where(same, c * scale, 0.0), BF16)
    cs = jnp.asarray(np.where(same, sn * scale, 0.0), BF16)
    head_mean = jnp.asarray(np.where(same, 1.0 / HEAD_DIM, 0.0), BF16)
    return cc, cs, head_mean


def _trunk(x, params, tables):
    (g_mix_pre, w_in, conv_w, g_conv_out, g_fourier_out, w_out, g_mix_post, g_mlp_pre, w_up,
     w_down, g_mlp_post) = params
    l1, l2, cc, cs, head_mean = tables
    depth = w_in.shape[0]
    for layer in range(depth):
        gb, z, u = _inproj(x, g_mix_pre, w_in, layer, l1.shape[0])
        yf = _seq_dft(u, l1, l2, cc, cs)
        x = _mix_mlp(x, gb, z, yf, conv_w, g_conv_out, g_fourier_out, head_mean, w_out,
                     g_mix_post, g_mlp_pre, w_up, w_down, g_mlp_post, layer)
    return x


def kernel(x_prompt, x_sample, g_mix_pre, w_in, conv_w, g_conv_out, g_fourier_out, w_out,
           g_mix_post, g_mlp_pre, w_up, w_down, g_mlp_post):
    depth = w_in.shape[0]
    vec3 = lambda g: g.reshape(depth, 1, g.shape[-1])
    params = (vec3(g_mix_pre), w_in.astype(BF16), conv_w, vec3(g_conv_out), vec3(g_fourier_out),
              w_out.astype(BF16), vec3(g_mix_post), vec3(g_mlp_pre), w_up.astype(BF16),
              w_down.astype(BF16), vec3(g_mlp_post))
    cc, cs, head_mean = _channel_tables(V7X_MXU_DIM)
    outs = []
    for x in (x_prompt, x_sample):
        l1, l2 = _seq_dft_tables(x.shape[1], SEQ_DFT_N1)
        outs.append(_trunk(x, params, (l1, l2, cc, cs, head_mean)))
    return tuple(outs)
```

```python
import functools
import math

import jax
import jax.numpy as jnp
import numpy as np
from jax import lax
from jax.experimental import pallas as pl
from jax.experimental.pallas import tpu as pltpu

HEAD_DIM = 64
EPS = 1e-6
F32 = jnp.float32
BF16 = jnp.bfloat16
U32 = jnp.uint32

V7X_LANES = 128
V7X_SUBLANES = 8
V7X_MXU_DIM = 256
VMEM_LIMIT_BYTES = 56 * 1024 * 1024

SEQ_DFT_N1 = 128
INPROJ_TILE = 2048
INPROJ_SUB_ROWS = 256
MIX_TILE = 1024
MIX_SUB_ROWS = 256
DFT_CHUNK = V7X_SUBLANES
DFT_MAX_CHUNKS_PER_STEP = 4
FF_CHUNK = 1024


def _rms(x, g):
    ms = jnp.mean(x * x, axis=-1, keepdims=True)
    return x * lax.rsqrt(ms + EPS) * g


def _const_spec(block_shape, index):
    return pl.BlockSpec(block_shape, lambda *_: index, pipeline_mode=pl.Buffered(1))


def _params():
    return pltpu.CompilerParams(dimension_semantics=("arbitrary", "arbitrary"),
                                vmem_limit_bytes=VMEM_LIMIT_BYTES)


def _lane_tile(x, t):
    return x[:, t * V7X_LANES:(t + 1) * V7X_LANES]


def _inproj_body(x_ref, g_ref, w_ref, gb_ref, z_ref, u_ref, *, dc, n2):
    r = INPROJ_SUB_ROWS
    n_chunks, half, _, _ = u_ref.shape
    c = DFT_CHUNK
    rows = (r // n2) * c

    def norm(s):
        return _rms(x_ref[s * r:(s + 1) * r, :], g_ref[...]).astype(BF16)

    def emit(s, p):
        gb_ref[s * r:(s + 1) * r, :] = p[:, :dc]
        z_ref[s * r:(s + 1) * r, :] = p[:, dc:2 * dc] * p[:, 2 * dc:3 * dc]
        bits = pltpu.bitcast(p[:, 3 * dc:].astype(BF16).astype(F32), U32)
        for t in range(half):
            word = _lane_tile(bits, t) | (_lane_tile(bits, t + half) >> 16)
            word = word.reshape(rows // c, n_chunks, c, V7X_LANES)
            for j in range(n_chunks):
                u_ref[j, t, s * rows:(s + 1) * rows, :] = word[:, j].reshape(rows, V7X_LANES)

    h = norm(0)
    for s in range(x_ref.shape[0] // r):
        p = jnp.dot(h, w_ref[...], preferred_element_type=F32)
        if (s + 1) * r < x_ref.shape[0]:
            h = norm(s + 1)
        emit(s, p)


def _inproj(x, g, w, layer, n2):
    b, s, d = x.shape
    dc = w.shape[-1] // 4
    t = INPROJ_TILE
    n_lt = dc // V7X_LANES
    c = DFT_CHUNK
    n_chunks = n2 // c
    rows = (t // n2) * c
    tok = lambda width: pl.BlockSpec((None, t, width), lambda bi, i: (bi, i, 0))
    out = jax.ShapeDtypeStruct((b, s, dc), F32)
    return pl.pallas_call(
        functools.partial(_inproj_body, dc=dc, n2=n2),
        grid=(b, s // t),
        in_specs=[tok(d),
                  _const_spec((None, 1, d), (layer, 0, 0)),
                  _const_spec((None, d, 4 * dc), (layer, 0, 0))],
        out_specs=[tok(dc), tok(dc),
                   pl.BlockSpec((None, n_chunks, n_lt // 2, rows, V7X_LANES), lambda bi, i: (bi, 0, 0, i, 0))],
        out_shape=[out, out, jax.ShapeDtypeStruct((b, n_chunks, n_lt // 2, (s // n2) * c, V7X_LANES), U32)],
        compiler_params=_params(),
        name=f"inproj_s{s}",
    )(x, g, w)


def _seq_dft_body(u_ref, l1_ref, l2_ref, cc_ref, cs_ref, y_ref, g_scr, *, n1, n2, c, p):
    j = pl.program_id(1)
    steps1 = n2 // (c * p)
    half = u_ref.shape[1]
    n_lt = 2 * half

    @pl.when(j < steps1)
    def _():
        for q in range(p):
            for s in range(c):
                words = [u_ref[q, t, pl.ds(s, n1, stride=c), :] for t in range(half)]
                slab = jnp.concatenate([pltpu.bitcast(w & jnp.uint32(0xFFFF0000), F32) for w in words]
                                       + [pltpu.bitcast(w << 16, F32) for w in words], axis=1)
                o = jnp.dot(l1_ref[q * c + s], slab.astype(BF16), preferred_element_type=F32)
                packed = pltpu.bitcast(o.astype(BF16), U32)
                for t in range(n_lt):
                    g_scr[t, j * p + q, pl.ds(s, n1, stride=c), :] = _lane_tile(packed, t)

    @pl.when(j >= steps1)
    def _():
        k0 = (j - steps1) * (p * c)
        vr, vi = [], []
        for k in range(p * c):
            rows = pl.ds((k0 + k) * c, c)
            w = jnp.concatenate([g_scr[t, :, rows, :].reshape(n2, V7X_LANES) for t in range(n_lt)], axis=1)
            v = jnp.dot(l2_ref[...], pltpu.bitcast(w, BF16), preferred_element_type=F32)
            vr.append(v[:n2].astype(BF16))
            vi.append(v[n2:].astype(BF16))
        vr = jnp.concatenate(vr, axis=0)
        vi = jnp.concatenate(vi, axis=0)
        group = cc_ref.shape[0]
        tiles = group // V7X_LANES
        for g in range(n_lt // tiles):
            cols = slice(g * group, (g + 1) * group)
            y = (jnp.dot(vr[:, cols], cc_ref[...], preferred_element_type=F32)
                 + jnp.dot(vi[:, cols], cs_ref[...], preferred_element_type=F32))
            for q in range(p):
                for k in range(c):
                    yk = y[(q * c + k) * n2:(q * c + k + 1) * n2]
                    for t in range(tiles):
                        y_ref[q, g * tiles + t, pl.ds(k, n2, stride=c), :] = _lane_tile(yk, t)


def _dft_chunks_per_step(n1, n2, width):
    c = DFT_CHUNK
    g_bytes = n1 * n2 * width * 4
    for p in range(DFT_MAX_CHUNKS_PER_STEP, 1, -1):
        io_bytes = 2 * p * c * ((n1 // 2 + n2) * width * 4 + 2 * n1 * n1 * 2)
        temp_bytes = p * c * n2 * width * (2 + 2 + 4) + 2 * n1 * width * 4 * 2
        if g_bytes + io_bytes + temp_bytes <= VMEM_LIMIT_BYTES and n2 % (p * c) == 0 and n1 % (p * c) == 0:
            return p
    return 1


def _seq_dft(u, l1, l2, cc, cs):
    b, n_chunks, half, _, _ = u.shape
    n_lt = 2 * half
    n2, _, n1 = l1.shape
    c = DFT_CHUNK
    p = _dft_chunks_per_step(n1, n2, n_lt * V7X_LANES)
    steps1 = n_chunks // p
    steps2 = n1 // (c * p)
    return pl.pallas_call(
        functools.partial(_seq_dft_body, n1=n1, n2=n2, c=c, p=p),
        grid=(b, steps1 + steps2),
        in_specs=[pl.BlockSpec((None, p, half, n1 * c, V7X_LANES),
                               lambda bi, j: (bi, jnp.minimum(j, steps1 - 1), 0, 0, 0)),
                  pl.BlockSpec((p * c, 2 * n1, n1), lambda bi, j: (jnp.minimum(j, steps1 - 1), 0, 0)),
                  _const_spec(l2.shape, (0, 0)),
                  _const_spec(cc.shape, (0, 0)),
                  _const_spec(cs.shape, (0, 0))],
        out_specs=pl.BlockSpec((None, p, n_lt, n2 * c, V7X_LANES),
                               lambda bi, j: (bi, jnp.maximum(j - steps1, 0), 0, 0, 0)),
        out_shape=jax.ShapeDtypeStruct((b, n1 // c, n_lt, n2 * c, V7X_LANES), F32),
        scratch_shapes=[pltpu.VMEM((n_lt, n_chunks, n1 * c, V7X_LANES), U32)],
        compiler_params=_params(),
        name=f"seq_dft_s{n1 * n2}",
    )(u, l1, l2, cc, cs)


def _mix_mlp_body(x_ref, gb_ref, z_ref, zp_ref, zn_ref, yf_ref, cw_ref, gco_ref, gfo_ref,
                  hm_ref, wo_ref, gpost_ref, gpre2_ref, wu_ref, wd_ref, gpost2_ref, o_ref,
                  *, n_tiles, sub_rows, ff_chunk):
    i = pl.program_id(1)
    r = sub_rows
    n_sub = z_ref.shape[0] // r
    n_ff = wu_ref.shape[-1] // ff_chunk
    halo = V7X_SUBLANES

    def head_norm(y, g):
        sq = (y * y).astype(BF16)
        group = hm_ref.shape[0]
        ms = jnp.concatenate(
            [jnp.dot(sq[:, q * group:(q + 1) * group], hm_ref[...], preferred_element_type=F32)
             for q in range(y.shape[1] // group)], axis=1)
        return y * lax.rsqrt(ms + EPS) * g

    def mix_inputs(s):
        rows = slice(s * r, (s + 1) * r)
        z = z_ref[rows, :]
        row = lax.broadcasted_iota(jnp.int32, z.shape, 0)
        if s == 0:
            z_before = jnp.where(i > 0, zp_ref[halo - 1:halo, :], 0.0)
        else:
            z_before = z_ref[s * r - 1:s * r, :]
        if s == n_sub - 1:
            z_after = jnp.where(i < n_tiles - 1, zn_ref[0:1, :], 0.0)
        else:
            z_after = z_ref[(s + 1) * r:(s + 1) * r + 1, :]
        z_m1 = jnp.where(row == 0, z_before, pltpu.roll(z, 1, 0))
        z_p1 = jnp.where(row == r - 1, z_after, pltpu.roll(z, r - 1, 0))
        cw = cw_ref[...]
        conv = z_m1 * cw[0:1] + z * cw[1:2] + z_p1 * cw[2:3]
        yc = gb_ref[rows, :] * conv
        n_kchunks, n_lt = yf_ref.shape[:2]
        c = DFT_CHUNK
        k2_per_sub = r // (n_kchunks * c)
        yf = jnp.concatenate(
            [jnp.concatenate([yf_ref[kc, t, k2 * c:(k2 + 1) * c, :]
                              for k2 in range(s * k2_per_sub, (s + 1) * k2_per_sub)
                              for kc in range(n_kchunks)], axis=0)
             for t in range(n_lt)], axis=1)
        merged = jnp.concatenate([head_norm(yc, gco_ref[...]), head_norm(yf, gfo_ref[...])], axis=-1)
        return merged.astype(BF16)

    def mix_out(s, merged):
        m = jnp.dot(merged, wo_ref[...], preferred_element_type=F32)
        x1 = x_ref[s * r:(s + 1) * r, :] + _rms(m, gpost_ref[...])
        return x1, _rms(x1, gpre2_ref[...]).astype(BF16)

    def mlp_chunk(h2, c):
        cols = slice(c * ff_chunk, (c + 1) * ff_chunk)
        a = jnp.maximum(jnp.dot(h2, wu_ref[:, cols], preferred_element_type=F32), 0.0)
        return jnp.dot((a * a).astype(BF16), wd_ref[cols, :], preferred_element_type=F32)

    x1, h2 = mix_out(0, mix_inputs(0))
    for s in range(n_sub):
        f = mlp_chunk(h2, 0)
        if s + 1 < n_sub:
            merged = mix_inputs(s + 1)
        f = f + mlp_chunk(h2, 1)
        if s + 1 < n_sub:
            nxt = mix_out(s + 1, merged)
        for c in range(2, n_ff):
            f = f + mlp_chunk(h2, c)
        o_ref[s * r:(s + 1) * r, :] = x1 + _rms(f, gpost2_ref[...])
        if s + 1 < n_sub:
            x1, h2 = nxt


def _mix_mlp(x, gb, z, yf, conv_w, g_co, g_fo, head_mean, w_out, g_post, g_pre2, w_up, w_down,
             g_post2, layer):
    b, s, d = x.shape
    dc = gb.shape[-1]
    d_ff = w_up.shape[-1]
    t = MIX_TILE
    n_tiles = s // t
    halo = V7X_SUBLANES
    hpt = t // halo
    n_halo = s // halo
    n_kchunks, n_lt = yf.shape[1:3]
    yf_rows = t // n_kchunks
    tok = lambda width: pl.BlockSpec((None, t, width), lambda bi, i: (bi, i, 0))
    vec = lambda width: _const_spec((None, 1, width), (layer, 0, 0))
    return pl.pallas_call(
        functools.partial(_mix_mlp_body, n_tiles=n_tiles, sub_rows=MIX_SUB_ROWS, ff_chunk=FF_CHUNK),
        grid=(b, n_tiles),
        in_specs=[tok(d), tok(dc), tok(dc),
                  pl.BlockSpec((None, halo, dc), lambda bi, i: (bi, jnp.maximum(i * hpt - 1, 0), 0)),
                  pl.BlockSpec((None, halo, dc), lambda bi, i: (bi, jnp.minimum((i + 1) * hpt, n_halo - 1), 0)),
                  pl.BlockSpec((None, n_kchunks, n_lt, yf_rows, V7X_LANES), lambda bi, i: (bi, 0, 0, i, 0)),
                  _const_spec((None, 3, dc), (layer, 0, 0)),
                  vec(dc), vec(dc),
                  _const_spec(head_mean.shape, (0, 0)),
                  _const_spec((None, d, d), (layer, 0, 0)),
                  vec(d), vec(d),
                  _const_spec((None, d, d_ff), (layer, 0, 0)),
                  _const_spec((None, d_ff, d), (layer, 0, 0)),
                  vec(d)],
        out_specs=tok(d),
        out_shape=jax.ShapeDtypeStruct((b, s, d), F32),
        compiler_params=_params(),
        name=f"mix_mlp_s{s}",
    )(x, gb, z, z, z, yf, conv_w, g_co, g_fo, head_mean, w_out, g_post, g_pre2, w_up, w_down, g_post2)


def _cos_sin(rows, cols, period):
    m = np.outer(np.arange(rows), np.arange(cols)) % period
    theta = m * (2.0 * np.pi / period)
    return np.cos(theta), np.sin(theta)


def _seq_dft_tables(s, n1):
    n2 = s // n1
    ca, sa = (t * (1.0 / math.sqrt(n1)) for t in _cos_sin(n1, n1, n1))
    cb, sb = _cos_sin(n2, n1, s)
    interleave = lambda re, im: jnp.asarray(np.stack([re, im], axis=1).reshape(2 * n1, n1), F32)
    a_cos, a_sin = interleave(ca, -sa), interleave(-sa, -ca)
    b_cos, b_sin = (jnp.asarray(np.repeat(t, 2, axis=1), F32)[:, :, None] for t in (cb, sb))
    l1 = (a_cos[None] * b_cos + a_sin[None] * b_sin).astype(BF16)
    c2, s2 = _cos_sin(n2, n2, n2)
    l2 = np.stack([np.stack([c2, s2], axis=-1), np.stack([-s2, c2], axis=-1)], axis=0)
    l2 = jnp.asarray(l2.reshape(2 * n2, 2 * n2) * (1.0 / math.sqrt(n2)), BF16)
    return l1, l2


def _channel_tables(group):
    same = np.equal.outer(np.arange(group) // HEAD_DIM, np.arange(group) // HEAD_DIM)
    c, sn = _cos_sin(group, group, HEAD_DIM)
    scale = 1.0 / math.sqrt(HEAD_DIM)
    cc = jnp.asarray(np.where(same, c * scale, 0.0), BF16)
    cs = jnp.asarray(np.where(same, sn * scale, 0.0), BF16)
    head_mean = jnp.asarray(np.where(same, 1.0 / HEAD_DIM, 0.0), BF16)
    return cc, cs, head_mean


def _trunk(x, params, tables):
    (g_mix_pre, w_in, conv_w, g_conv_out, g_fourier_out, w_out, g_mix_post, g_mlp_pre, w_up,
     w_down, g_mlp_post) = params
    l1, l2, cc, cs, head_mean = tables
    depth = w_in.shape[0]
    for layer in range(depth):
        gb, z, u = _inproj(x, g_mix_pre, w_in, layer, l1.shape[0])
        yf = _seq_dft(u, l1, l2, cc, cs)
        x = _mix_mlp(x, gb, z, yf, conv_w, g_conv_out, g_fourier_out, head_mean, w_out,
                     g_mix_post, g_mlp_pre, w_up, w_down, g_mlp_post, layer)
    return x


def kernel(x_prompt, x_sample, g_mix_pre, w_in, conv_w, g_conv_out, g_fourier_out, w_out,
           g_mix_post, g_mlp_pre, w_up, w_down, g_mlp_post):
    depth = w_in.shape[0]
    vec3 = lambda g: g.reshape(depth, 1, g.shape[-1])
    params = (vec3(g_mix_pre), w_in.astype(BF16), conv_w, vec3(g_conv_out), vec3(g_fourier_out),
              w_out.astype(BF16), vec3(g_mix_post), vec3(g_mlp_pre), w_up.astype(BF16),
              w_down.astype(BF16), vec3(g_mlp_post))
    cc, cs, head_mean = _channel_tables(V7X_MXU_DIM)
    outs = []
    for x in (x_prompt, x_sample):
        l1, l2 = _seq_dft_tables(x.shape[1], SEQ_DFT_N1)
        outs.append(_trunk(x, params, (l1, l2, cc, cs, head_mean)))
    return tuple(outs)
```

```python
import functools
import math

import jax
import jax.numpy as jnp
import numpy as np
from jax import lax
from jax.experimental import pallas as pl
from jax.experimental.pallas import tpu as pltpu

HEAD_DIM = 64
EPS = 1e-6
F32 = jnp.float32
BF16 = jnp.bfloat16
U32 = jnp.uint32

V7X_LANES = 128
V7X_SUBLANES = 8
V7X_MXU_DIM = 256
VMEM_LIMIT_BYTES = 56 * 1024 * 1024

SEQ_DFT_N1 = 128
INPROJ_TILE = 2048
INPROJ_SUB_ROWS = 256
MIX_TILE = 1024
MIX_SUB_ROWS = 256
DFT_CHUNK = V7X_SUBLANES
DFT_MAX_CHUNKS_PER_STEP = 4
FF_CHUNK = 1024


def _rms(x, g):
    ms = jnp.mean(x * x, axis=-1, keepdims=True)
    return x * lax.rsqrt(ms + EPS) * g


def _const_spec(block_shape, index):
    return pl.BlockSpec(block_shape, lambda *_: index, pipeline_mode=pl.Buffered(1))


def _params():
    return pltpu.CompilerParams(dimension_semantics=("arbitrary", "arbitrary"),
                                vmem_limit_bytes=VMEM_LIMIT_BYTES)


def _lane_tile(x, t):
    return x[:, t * V7X_LANES:(t + 1) * V7X_LANES]


def _inproj_body(x_ref, g_ref, w_ref, gb_ref, z_ref, u_ref, *, dc, n2):
    r = INPROJ_SUB_ROWS
    n_chunks, half, _, _ = u_ref.shape
    c = DFT_CHUNK
    rows = (r // n2) * c

    def norm(s):
        return _rms(x_ref[s * r:(s + 1) * r, :], g_ref[...]).astype(BF16)

    def emit(s, p):
        gb_ref[s * r:(s + 1) * r, :] = p[:, :dc]
        z_ref[s * r:(s + 1) * r, :] = p[:, dc:2 * dc] * p[:, 2 * dc:3 * dc]
        bits = pltpu.bitcast(p[:, 3 * dc:].astype(BF16).astype(F32), U32)
        for t in range(half):
            word = _lane_tile(bits, t) | (_lane_tile(bits, t + half) >> 16)
            word = word.reshape(rows // c, n_chunks, c, V7X_LANES)
            for j in range(n_chunks):
                u_ref[j, t, s * rows:(s + 1) * rows, :] = word[:, j].reshape(rows, V7X_LANES)

    h = norm(0)
    for s in range(x_ref.shape[0] // r):
        p = jnp.dot(h, w_ref[...], preferred_element_type=F32)
        if (s + 1) * r < x_ref.shape[0]:
            h = norm(s + 1)
        emit(s, p)


def _inproj(x, g, w, layer, n2):
    b, s, d = x.shape
    dc = w.shape[-1] // 4
    t = INPROJ_TILE
    n_lt = dc // V7X_LANES
    c = DFT_CHUNK
    n_chunks = n2 // c
    rows = (t // n2) * c
    assert s % t == 0 and t % INPROJ_SUB_ROWS == 0 and INPROJ_SUB_ROWS % n2 == 0 and n2 % c == 0, (s, n2)
    tok = lambda width: pl.BlockSpec((None, t, width), lambda bi, i: (bi, i, 0))
    out = jax.ShapeDtypeStruct((b, s, dc), F32)
    return pl.pallas_call(
        functools.partial(_inproj_body, dc=dc, n2=n2),
        grid=(b, s // t),
        in_specs=[tok(d),
                  _const_spec((None, 1, d), (layer, 0, 0)),
                  _const_spec((None, d, 4 * dc), (layer, 0, 0))],
        out_specs=[tok(dc), tok(dc),
                   pl.BlockSpec((None, n_chunks, n_lt // 2, rows, V7X_LANES), lambda bi, i: (bi, 0, 0, i, 0))],
        out_shape=[out, out, jax.ShapeDtypeStruct((b, n_chunks, n_lt // 2, (s // n2) * c, V7X_LANES), U32)],
        compiler_params=_params(),
        name=f"inproj_s{s}",
    )(x, g, w)


def _seq_dft_body(u_ref, l1_ref, l2_ref, cc_ref, cs_ref, y_ref, g_scr, *, n1, n2, c, p):
    j = pl.program_id(1)
    steps1 = n2 // (c * p)
    half = u_ref.shape[1]
    n_lt = 2 * half

    @pl.when(j < steps1)
    def _():
        for q in range(p):
            for s in range(c):
                words = [u_ref[q, t, pl.ds(s, n1, stride=c), :] for t in range(half)]
                slab = jnp.concatenate([pltpu.bitcast(w & jnp.uint32(0xFFFF0000), F32) for w in words]
                                       + [pltpu.bitcast(w << 16, F32) for w in words], axis=1)
                o = jnp.dot(l1_ref[q * c + s], slab.astype(BF16), preferred_element_type=F32)
                packed = pltpu.bitcast(o.astype(BF16), U32)
                for t in range(n_lt):
                    g_scr[t, j * p + q, pl.ds(s, n1, stride=c), :] = _lane_tile(packed, t)

    @pl.when(j >= steps1)
    def _():
        k0 = (j - steps1) * (p * c)
        vr, vi = [], []
        for k in range(p * c):
            rows = pl.ds((k0 + k) * c, c)
            w = jnp.concatenate([g_scr[t, :, rows, :].reshape(n2, V7X_LANES) for t in range(n_lt)], axis=1)
            v = jnp.dot(l2_ref[...], pltpu.bitcast(w, BF16), preferred_element_type=F32)
            vr.append(v[:n2].astype(BF16))
            vi.append(v[n2:].astype(BF16))
        vr = jnp.concatenate(vr, axis=0)
        vi = jnp.concatenate(vi, axis=0)
        group = cc_ref.shape[0]
        tiles = group // V7X_LANES
        for g in range(n_lt // tiles):
            cols = slice(g * group, (g + 1) * group)
            y = (jnp.dot(vr[:, cols], cc_ref[...], preferred_element_type=F32)
                 + jnp.dot(vi[:, cols], cs_ref[...], preferred_element_type=F32))
            for q in range(p):
                for k in range(c):
                    yk = y[(q * c + k) * n2:(q * c + k + 1) * n2]
                    for t in range(tiles):
                        y_ref[q, g * tiles + t, pl.ds(k, n2, stride=c), :] = _lane_tile(yk, t)


def _dft_chunks_per_step(n1, n2, width):
    c = DFT_CHUNK
    g_bytes = n1 * n2 * width * 4
    for p in range(DFT_MAX_CHUNKS_PER_STEP, 1, -1):
        io_bytes = 2 * p * c * ((n1 // 2 + n2) * width * 4 + 2 * n1 * n1 * 2)
        temp_bytes = p * c * n2 * width * (2 + 2 + 4) + 2 * n1 * width * 4 * 2
        if g_bytes + io_bytes + temp_bytes <= VMEM_LIMIT_BYTES and n2 % (p * c) == 0 and n1 % (p * c) == 0:
            return p
    return 1


def _seq_dft(u, l1, l2, cc, cs):
    b, n_chunks, half, _, _ = u.shape
    n_lt = 2 * half
    n2, _, n1 = l1.shape
    c = DFT_CHUNK
    p = _dft_chunks_per_step(n1, n2, n_lt * V7X_LANES)
    assert n_chunks * c == n2 and n2 % (c * p) == 0 and n1 % (c * p) == 0, (n1, n2, p)
    steps1 = n_chunks // p
    steps2 = n1 // (c * p)
    return pl.pallas_call(
        functools.partial(_seq_dft_body, n1=n1, n2=n2, c=c, p=p),
        grid=(b, steps1 + steps2),
        in_specs=[pl.BlockSpec((None, p, half, n1 * c, V7X_LANES),
                               lambda bi, j: (bi, jnp.minimum(j, steps1 - 1), 0, 0, 0)),
                  pl.BlockSpec((p * c, 2 * n1, n1), lambda bi, j: (jnp.minimum(j, steps1 - 1), 0, 0)),
                  _const_spec(l2.shape, (0, 0)),
                  _const_spec(cc.shape, (0, 0)),
                  _const_spec(cs.shape, (0, 0))],
        out_specs=pl.BlockSpec((None, p, n_lt, n2 * c, V7X_LANES),
                               lambda bi, j: (bi, jnp.maximum(j - steps1, 0), 0, 0, 0)),
        out_shape=jax.ShapeDtypeStruct((b, n1 // c, n_lt, n2 * c, V7X_LANES), F32),
        scratch_shapes=[pltpu.VMEM((n_lt, n_chunks, n1 * c, V7X_LANES), U32)],
        compiler_params=_params(),
        name=f"seq_dft_s{n1 * n2}",
    )(u, l1, l2, cc, cs)


def _mix_mlp_body(x_ref, gb_ref, z_ref, zp_ref, zn_ref, yf_ref, cw_ref, gco_ref, gfo_ref,
                  hm_ref, wo_ref, gpost_ref, gpre2_ref, wu_ref, wd_ref, gpost2_ref, o_ref,
                  *, n_tiles, sub_rows, ff_chunk):
    i = pl.program_id(1)
    r = sub_rows
    n_sub = z_ref.shape[0] // r
    n_ff = wu_ref.shape[-1] // ff_chunk
    halo = V7X_SUBLANES

    def head_norm(y, g):
        sq = (y * y).astype(BF16)
        group = hm_ref.shape[0]
        ms = jnp.concatenate(
            [jnp.dot(sq[:, q * group:(q + 1) * group], hm_ref[...], preferred_element_type=F32)
             for q in range(y.shape[1] // group)], axis=1)
        return y * lax.rsqrt(ms + EPS) * g

    def mix_inputs(s):
        rows = slice(s * r, (s + 1) * r)
        z = z_ref[rows, :]
        row = lax.broadcasted_iota(jnp.int32, z.shape, 0)
        if s == 0:
            z_before = jnp.where(i > 0, zp_ref[halo - 1:halo, :], 0.0)
        else:
            z_before = z_ref[s * r - 1:s * r, :]
        if s == n_sub - 1:
            z_after = jnp.where(i < n_tiles - 1, zn_ref[0:1, :], 0.0)
        else:
            z_after = z_ref[(s + 1) * r:(s + 1) * r + 1, :]
        z_m1 = jnp.where(row == 0, z_before, pltpu.roll(z, 1, 0))
        z_p1 = jnp.where(row == r - 1, z_after, pltpu.roll(z, r - 1, 0))
        cw = cw_ref[...]
        conv = z_m1 * cw[0:1] + z * cw[1:2] + z_p1 * cw[2:3]
        yc = gb_ref[rows, :] * conv
        n_kchunks, n_lt = yf_ref.shape[:2]
        c = DFT_CHUNK
        k2_per_sub = r // (n_kchunks * c)
        yf = jnp.concatenate(
            [jnp.concatenate([yf_ref[kc, t, k2 * c:(k2 + 1) * c, :]
                              for k2 in range(s * k2_per_sub, (s + 1) * k2_per_sub)
                              for kc in range(n_kchunks)], axis=0)
             for t in range(n_lt)], axis=1)
        merged = jnp.concatenate([head_norm(yc, gco_ref[...]), head_norm(yf, gfo_ref[...])], axis=-1)
        return merged.astype(BF16)

    def mix_out(s, merged):
        m = jnp.dot(merged, wo_ref[...], preferred_element_type=F32)
        x1 = x_ref[s * r:(s + 1) * r, :] + _rms(m, gpost_ref[...])
        return x1, _rms(x1, gpre2_ref[...]).astype(BF16)

    def mlp_chunk(h2, c):
        cols = slice(c * ff_chunk, (c + 1) * ff_chunk)
        a = jnp.maximum(jnp.dot(h2, wu_ref[:, cols], preferred_element_type=F32), 0.0)
        return jnp.dot((a * a).astype(BF16), wd_ref[cols, :], preferred_element_type=F32)

    x1, h2 = mix_out(0, mix_inputs(0))
    for s in range(n_sub):
        f = mlp_chunk(h2, 0)
        if s + 1 < n_sub:
            merged = mix_inputs(s + 1)
        f = f + mlp_chunk(h2, 1)
        if s + 1 < n_sub:
            nxt = mix_out(s + 1, merged)
        for c in range(2, n_ff):
            f = f + mlp_chunk(h2, c)
        o_ref[s * r:(s + 1) * r, :] = x1 + _rms(f, gpost2_ref[...])
        if s + 1 < n_sub:
            x1, h2 = nxt


def _mix_mlp(x, gb, z, yf, conv_w, g_co, g_fo, head_mean, w_out, g_post, g_pre2, w_up, w_down,
             g_post2, layer):
    b, s, d = x.shape
    dc = gb.shape[-1]
    d_ff = w_up.shape[-1]
    t = MIX_TILE
    n_tiles = s // t
    halo = V7X_SUBLANES
    hpt = t // halo
    n_halo = s // halo
    n_kchunks, n_lt = yf.shape[1:3]
    yf_rows = t // n_kchunks
    assert s % t == 0 and t % MIX_SUB_ROWS == 0 and MIX_SUB_ROWS % (n_kchunks * DFT_CHUNK) == 0, (s, n_kchunks)
    tok = lambda width: pl.BlockSpec((None, t, width), lambda bi, i: (bi, i, 0))
    vec = lambda width: _const_spec((None, 1, width), (layer, 0, 0))
    return pl.pallas_call(
        functools.partial(_mix_mlp_body, n_tiles=n_tiles, sub_rows=MIX_SUB_ROWS, ff_chunk=FF_CHUNK),
        grid=(b, n_tiles),
        in_specs=[tok(d), tok(dc), tok(dc),
                  pl.BlockSpec((None, halo, dc), lambda bi, i: (bi, jnp.maximum(i * hpt - 1, 0), 0)),
                  pl.BlockSpec((None, halo, dc), lambda bi, i: (bi, jnp.minimum((i + 1) * hpt, n_halo - 1), 0)),
                  pl.BlockSpec((None, n_kchunks, n_lt, yf_rows, V7X_LANES), lambda bi, i: (bi, 0, 0, i, 0)),
                  _const_spec((None, 3, dc), (layer, 0, 0)),
                  vec(dc), vec(dc),
                  _const_spec(head_mean.shape, (0, 0)),
                  _const_spec((None, d, d), (layer, 0, 0)),
                  vec(d), vec(d),
                  _const_spec((None, d, d_ff), (layer, 0, 0)),
                  _const_spec((None, d_ff, d), (layer, 0, 0)),
                  vec(d)],
        out_specs=tok(d),
        out_shape=jax.ShapeDtypeStruct((b, s, d), F32),
        compiler_params=_params(),
        name=f"mix_mlp_s{s}",
    )(x, gb, z, z, z, yf, conv_w, g_co, g_fo, head_mean, w_out, g_post, g_pre2, w_up, w_down, g_post2)


def _cos_sin(rows, cols, period):
    m = np.outer(np.arange(rows), np.arange(cols)) % period
    theta = m * (2.0 * np.pi / period)
    return np.cos(theta), np.sin(theta)


def _seq_dft_tables(s, n1):
    assert s % n1 == 0, (s, n1)
    n2 = s // n1
    ca, sa = (t * (1.0 / math.sqrt(n1)) for t in _cos_sin(n1, n1, n1))
    cb, sb = _cos_sin(n2, n1, s)
    interleave = lambda re, im: jnp.asarray(np.stack([re, im], axis=1).reshape(2 * n1, n1), F32)
    a_cos, a_sin = interleave(ca, -sa), interleave(-sa, -ca)
    b_cos, b_sin = (jnp.asarray(np.repeat(t, 2, axis=1), F32)[:, :, None] for t in (cb, sb))
    l1 = (a_cos[None] * b_cos + a_sin[None] * b_sin).astype(BF16)
    c2, s2 = _cos_sin(n2, n2, n2)
    l2 = np.stack([np.stack([c2, s2], axis=-1), np.stack([-s2, c2], axis=-1)], axis=0)
    l2 = jnp.asarray(l2.reshape(2 * n2, 2 * n2) * (1.0 / math.sqrt(n2)), BF16)
    return l1, l2


def _channel_tables(group):
    same = np.equal.outer(np.arange(group) // HEAD_DIM, np.arange(group) // HEAD_DIM)
    c, sn = _cos_sin(group, group, HEAD_DIM)
    scale = 1.0 / math.sqrt(HEAD_DIM)
    cc = jnp.asarray(np.where(same, c * scale, 0.0), BF16)
    cs = jnp.asarray(np.where(same, sn * scale, 0.0), BF16)
    head_mean = jnp.asarray(np.where(same, 1.0 / HEAD_DIM, 0.0), BF16)
    return cc, cs, head_mean


def _trunk(x, params, tables):
    (g_mix_pre, w_in, conv_w, g_conv_out, g_fourier_out, w_out, g_mix_post, g_mlp_pre, w_up,
     w_down, g_mlp_post) = params
    l1, l2, cc, cs, head_mean = tables
    depth = w_in.shape[0]
    for layer in range(depth):
        gb, z, u = _inproj(x, g_mix_pre, w_in, layer, l1.shape[0])
        yf = _seq_dft(u, l1, l2, cc, cs)
        x = _mix_mlp(x, gb, z, yf, conv_w, g_conv_out, g_fourier_out, head_mean, w_out,
                     g_mix_post, g_mlp_pre, w_up, w_down, g_mlp_post, layer)
    return x


def kernel(x_prompt, x_sample, g_mix_pre, w_in, conv_w, g_conv_out, g_fourier_out, w_out,
           g_mix_post, g_mlp_pre, w_up, w_down, g_mlp_post):
    depth = w_in.shape[0]
    vec3 = lambda g: g.reshape(depth, 1, g.shape[-1])
    params = (vec3(g_mix_pre), w_in.astype(BF16), conv_w, vec3(g_conv_out), vec3(g_fourier_out),
              w_out.astype(BF16), vec3(g_mix_post), vec3(g_mlp_pre), w_up.astype(BF16),
              w_down.astype(BF16), vec3(g_mlp_post))
    cc, cs, head_mean = _channel_tables(V7X_MXU_DIM)
    outs = []
    for x in (x_prompt, x_sample):
        l1, l2 = _seq_dft_tables(x.shape[1], SEQ_DFT_N1)
        outs.append(_trunk(x, params, (l1, l2, cc, cs, head_mean)))
    return tuple(outs)
```

```python
import functools
import math

import jax
import jax.numpy as jnp
import numpy as np
from jax import lax
from jax.experimental import pallas as pl
from jax.experimental.pallas import tpu as pltpu

HEAD_DIM = 64
EPS = 1e-6
F32 = jnp.float32
BF16 = jnp.bfloat16
U32 = jnp.uint32

V7X_LANES = 128
V7X_SUBLANES = 8
V7X_MXU_DIM = 256
VMEM_LIMIT_BYTES = 56 * 1024 * 1024

SEQ_DFT_N1 = 128
INPROJ_TILE = 2048
INPROJ_SUB_ROWS = 256
MIX_TILE = 1024
MIX_SUB_ROWS = 256
DFT_CHUNK = V7X_SUBLANES
DFT_MAX_CHUNKS_PER_STEP = 4
FF_CHUNK = 1024


def _rms(x, g):
    ms = jnp.mean(x * x, axis=-1, keepdims=True)
    return x * lax.rsqrt(ms + EPS) * g


def _const_spec(block_shape, index):
    return pl.BlockSpec(block_shape, lambda *_: index, pipeline_mode=pl.Buffered(1))


def _params():
    return pltpu.CompilerParams(dimension_semantics=("arbitrary", "arbitrary"),
                                vmem_limit_bytes=VMEM_LIMIT_BYTES)


def _lane_tile(x, t):
    return x[:, t * V7X_LANES:(t + 1) * V7X_LANES]


def _inproj_body(x_ref, g_ref, w_ref, *refs, dc, n2):
    cast = len(refs) == 7
    gb_ref, z_ref, u_ref = refs[2:5] if cast else refs
    r = INPROJ_SUB_ROWS
    n_chunks, half, _, _ = u_ref.shape
    c = DFT_CHUNK
    rows = (r // n2) * c

    def norm(s):
        return _rms(x_ref[s * r:(s + 1) * r, :], g_ref[...]).astype(BF16)

    def emit(s, p):
        gb_ref[s * r:(s + 1) * r, :] = p[:, :dc]
        z_ref[s * r:(s + 1) * r, :] = p[:, dc:2 * dc] * p[:, 2 * dc:3 * dc]
        bits = pltpu.bitcast(p[:, 3 * dc:].astype(BF16).astype(F32), U32)
        for t in range(half):
            word = _lane_tile(bits, t) | (_lane_tile(bits, t + half) >> 16)
            word = word.reshape(rows // c, n_chunks, c, V7X_LANES)
            for j in range(n_chunks):
                u_ref[j, t, s * rows:(s + 1) * rows, :] = word[:, j].reshape(rows, V7X_LANES)

    h = norm(0)
    for s in range(x_ref.shape[0] // r):
        p = jnp.dot(h, w_ref[...], preferred_element_type=F32)
        if (s + 1) * r < x_ref.shape[0]:
            h = norm(s + 1)
        emit(s, p)
    if cast:
        refs[5][...] = refs[0][...].astype(BF16)
        refs[6][...] = refs[1][...].astype(BF16)


def _inproj(x, g, w, layer, n2, ff_f32=None):
    b, s, d = x.shape
    dc = w.shape[-1] // 4
    t = INPROJ_TILE
    n_lt = dc // V7X_LANES
    c = DFT_CHUNK
    n_chunks = n2 // c
    rows = (t // n2) * c
    assert s % t == 0 and t % INPROJ_SUB_ROWS == 0 and INPROJ_SUB_ROWS % n2 == 0 and n2 % c == 0, (s, n2)
    tok = lambda width: pl.BlockSpec((None, t, width), lambda bi, i: (bi, i, 0))
    out = jax.ShapeDtypeStruct((b, s, dc), F32)
    in_specs = [tok(d),
                _const_spec((None, 1, d), (layer, 0, 0)),
                _const_spec((None, d, 4 * dc), (layer, 0, 0))]
    out_specs = [tok(dc), tok(dc),
                 pl.BlockSpec((None, n_chunks, n_lt // 2, rows, V7X_LANES), lambda bi, i: (bi, 0, 0, i, 0))]
    out_shape = [out, out, jax.ShapeDtypeStruct((b, n_chunks, n_lt // 2, (s // n2) * c, V7X_LANES), U32)]
    operands = [x, g, w]
    if ff_f32 is not None:
        n_t = s // t
        steps = b * n_t
        for wf in ff_f32:
            _, rows_w, cols_w = wf.shape
            assert rows_w % (steps * 2 * V7X_SUBLANES) == 0, (wf.shape, steps)
            in_specs.append(pl.BlockSpec((None, rows_w // steps, cols_w), lambda bi, i: (layer, bi * n_t + i, 0)))
            out_specs.append(pl.BlockSpec((rows_w // steps, cols_w), lambda bi, i: (bi * n_t + i, 0)))
            out_shape.append(jax.ShapeDtypeStruct((rows_w, cols_w), BF16))
            operands.append(wf)
    return pl.pallas_call(
        functools.partial(_inproj_body, dc=dc, n2=n2),
        grid=(b, s // t),
        in_specs=in_specs,
        out_specs=out_specs,
        out_shape=out_shape,
        compiler_params=_params(),
        name=f"inproj_s{s}",
    )(*operands)


def _seq_dft_body(u_ref, l1_ref, l2_ref, cc_ref, cs_ref, y_ref, g_scr, *, n1, n2, c, p):
    j = pl.program_id(1)
    steps1 = n2 // (c * p)
    half = u_ref.shape[1]
    n_lt = 2 * half

    @pl.when(j < steps1)
    def _():
        for q in range(p):
            for s in range(c):
                words = [u_ref[q, t, pl.ds(s, n1, stride=c), :] for t in range(half)]
                slab = jnp.concatenate([pltpu.bitcast(w & jnp.uint32(0xFFFF0000), F32) for w in words]
                                       + [pltpu.bitcast(w << 16, F32) for w in words], axis=1)
                o = jnp.dot(l1_ref[q * c + s], slab.astype(BF16), preferred_element_type=F32)
                packed = pltpu.bitcast(o.astype(BF16), U32)
                for t in range(n_lt):
                    g_scr[t, j * p + q, pl.ds(s, n1, stride=c), :] = _lane_tile(packed, t)

    @pl.when(j >= steps1)
    def _():
        k0 = (j - steps1) * (p * c)
        vr, vi = [], []
        for k in range(p * c):
            rows = pl.ds((k0 + k) * c, c)
            w = jnp.concatenate([g_scr[t, :, rows, :].reshape(n2, V7X_LANES) for t in range(n_lt)], axis=1)
            v = jnp.dot(l2_ref[...], pltpu.bitcast(w, BF16), preferred_element_type=F32)
            vr.append(v[:n2].astype(BF16))
            vi.append(v[n2:].astype(BF16))
        vr = jnp.concatenate(vr, axis=0)
        vi = jnp.concatenate(vi, axis=0)
        group = cc_ref.shape[0]
        tiles = group // V7X_LANES
        for g in range(n_lt // tiles):
            cols = slice(g * group, (g + 1) * group)
            y = (jnp.dot(vr[:, cols], cc_ref[...], preferred_element_type=F32)
                 + jnp.dot(vi[:, cols], cs_ref[...], preferred_element_type=F32))
            for q in range(p):
                for k in range(c):
                    yk = y[(q * c + k) * n2:(q * c + k + 1) * n2]
                    for t in range(tiles):
                        y_ref[q, g * tiles + t, pl.ds(k, n2, stride=c), :] = _lane_tile(yk, t)


def _dft_chunks_per_step(n1, n2, width):
    c = DFT_CHUNK
    g_bytes = n1 * n2 * width * 4
    for p in range(DFT_MAX_CHUNKS_PER_STEP, 1, -1):
        io_bytes = 2 * p * c * ((n1 // 2 + n2) * width * 4 + 2 * n1 * n1 * 2)
        temp_bytes = p * c * n2 * width * (2 + 2 + 4) + 2 * n1 * width * 4 * 2
        if g_bytes + io_bytes + temp_bytes <= VMEM_LIMIT_BYTES and n2 % (p * c) == 0 and n1 % (p * c) == 0:
            return p
    return 1


def _seq_dft(u, l1, l2, cc, cs):
    b, n_chunks, half, _, _ = u.shape
    n_lt = 2 * half
    n2, _, n1 = l1.shape
    c = DFT_CHUNK
    p = _dft_chunks_per_step(n1, n2, n_lt * V7X_LANES)
    assert n_chunks * c == n2 and n2 % (c * p) == 0 and n1 % (c * p) == 0, (n1, n2, p)
    steps1 = n_chunks // p
    steps2 = n1 // (c * p)
    return pl.pallas_call(
        functools.partial(_seq_dft_body, n1=n1, n2=n2, c=c, p=p),
        grid=(b, steps1 + steps2),
        in_specs=[pl.BlockSpec((None, p, half, n1 * c, V7X_LANES),
                               lambda bi, j: (bi, jnp.minimum(j, steps1 - 1), 0, 0, 0)),
                  pl.BlockSpec((p * c, 2 * n1, n1), lambda bi, j: (jnp.minimum(j, steps1 - 1), 0, 0)),
                  _const_spec(l2.shape, (0, 0)),
                  _const_spec(cc.shape, (0, 0)),
                  _const_spec(cs.shape, (0, 0))],
        out_specs=pl.BlockSpec((None, p, n_lt, n2 * c, V7X_LANES),
                               lambda bi, j: (bi, jnp.maximum(j - steps1, 0), 0, 0, 0)),
        out_shape=jax.ShapeDtypeStruct((b, n1 // c, n_lt, n2 * c, V7X_LANES), F32),
        scratch_shapes=[pltpu.VMEM((n_lt, n_chunks, n1 * c, V7X_LANES), U32)],
        compiler_params=_params(),
        name=f"seq_dft_s{n1 * n2}",
    )(u, l1, l2, cc, cs)


def _mix_mlp_body(x_ref, gb_ref, z_ref, zp_ref, zn_ref, yf_ref, cw_ref, gco_ref, gfo_ref,
                  hm_ref, wo_ref, gpost_ref, gpre2_ref, wu_ref, wd_ref, gpost2_ref, o_ref,
                  *, n_tiles, sub_rows, ff_chunk):
    i = pl.program_id(1)
    r = sub_rows
    n_sub = z_ref.shape[0] // r
    n_ff = wu_ref.shape[-1] // ff_chunk
    halo = V7X_SUBLANES

    def head_norm(y, g):
        sq = (y * y).astype(BF16)
        group = hm_ref.shape[0]
        ms = jnp.concatenate(
            [jnp.dot(sq[:, q * group:(q + 1) * group], hm_ref[...], preferred_element_type=F32)
             for q in range(y.shape[1] // group)], axis=1)
        return y * lax.rsqrt(ms + EPS) * g

    def mix_inputs(s):
        rows = slice(s * r, (s + 1) * r)
        z = z_ref[rows, :]
        row = lax.broadcasted_iota(jnp.int32, z.shape, 0)
        if s == 0:
            z_before = jnp.where(i > 0, zp_ref[halo - 1:halo, :], 0.0)
        else:
            z_before = z_ref[s * r - 1:s * r, :]
        if s == n_sub - 1:
            z_after = jnp.where(i < n_tiles - 1, zn_ref[0:1, :], 0.0)
        else:
            z_after = z_ref[(s + 1) * r:(s + 1) * r + 1, :]
        z_m1 = jnp.where(row == 0, z_before, pltpu.roll(z, 1, 0))
        z_p1 = jnp.where(row == r - 1, z_after, pltpu.roll(z, r - 1, 0))
        cw = cw_ref[...]
        conv = z_m1 * cw[0:1] + z * cw[1:2] + z_p1 * cw[2:3]
        yc = gb_ref[rows, :] * conv
        n_kchunks, n_lt = yf_ref.shape[:2]
        c = DFT_CHUNK
        k2_per_sub = r // (n_kchunks * c)
        yf = jnp.concatenate(
            [jnp.concatenate([yf_ref[kc, t, k2 * c:(k2 + 1) * c, :]
                              for k2 in range(s * k2_per_sub, (s + 1) * k2_per_sub)
                              for kc in range(n_kchunks)], axis=0)
             for t in range(n_lt)], axis=1)
        merged = jnp.concatenate([head_norm(yc, gco_ref[...]), head_norm(yf, gfo_ref[...])], axis=-1)
        return merged.astype(BF16)

    def mix_out(s, merged):
        m = jnp.dot(merged, wo_ref[...], preferred_element_type=F32)
        x1 = x_ref[s * r:(s + 1) * r, :] + _rms(m, gpost_ref[...])
        return x1, _rms(x1, gpre2_ref[...]).astype(BF16)

    def mlp_chunk(h2, c):
        cols = slice(c * ff_chunk, (c + 1) * ff_chunk)
        a = jnp.maximum(jnp.dot(h2, wu_ref[:, cols], preferred_element_type=F32), 0.0)
        return jnp.dot((a * a).astype(BF16), wd_ref[cols, :], preferred_element_type=F32)

    x1, h2 = mix_out(0, mix_inputs(0))
    for s in range(n_sub):
        f = mlp_chunk(h2, 0)
        if s + 1 < n_sub:
            merged = mix_inputs(s + 1)
        f = f + mlp_chunk(h2, 1)
        if s + 1 < n_sub:
            nxt = mix_out(s + 1, merged)
        for c in range(2, n_ff):
            f = f + mlp_chunk(h2, c)
        o_ref[s * r:(s + 1) * r, :] = x1 + _rms(f, gpost2_ref[...])
        if s + 1 < n_sub:
            x1, h2 = nxt


def _mix_mlp(x, gb, z, yf, conv_w, g_co, g_fo, head_mean, w_out, g_post, g_pre2, w_up, w_down,
             g_post2, layer):
    b, s, d = x.shape
    dc = gb.shape[-1]
    d_ff = w_up.shape[-1]
    t = MIX_TILE
    n_tiles = s // t
    halo = V7X_SUBLANES
    hpt = t // halo
    n_halo = s // halo
    n_kchunks, n_lt = yf.shape[1:3]
    yf_rows = t // n_kchunks
    assert s % t == 0 and t % MIX_SUB_ROWS == 0 and MIX_SUB_ROWS % (n_kchunks * DFT_CHUNK) == 0, (s, n_kchunks)
    tok = lambda width: pl.BlockSpec((None, t, width), lambda bi, i: (bi, i, 0))
    vec = lambda width: _const_spec((None, 1, width), (layer, 0, 0))
    return pl.pallas_call(
        functools.partial(_mix_mlp_body, n_tiles=n_tiles, sub_rows=MIX_SUB_ROWS, ff_chunk=FF_CHUNK),
        grid=(b, n_tiles),
        in_specs=[tok(d), tok(dc), tok(dc),
                  pl.BlockSpec((None, halo, dc), lambda bi, i: (bi, jnp.maximum(i * hpt - 1, 0), 0)),
                  pl.BlockSpec((None, halo, dc), lambda bi, i: (bi, jnp.minimum((i + 1) * hpt, n_halo - 1), 0)),
                  pl.BlockSpec((None, n_kchunks, n_lt, yf_rows, V7X_LANES), lambda bi, i: (bi, 0, 0, i, 0)),
                  _const_spec((None, 3, dc), (layer, 0, 0)),
                  vec(dc), vec(dc),
                  _const_spec(head_mean.shape, (0, 0)),
                  _const_spec((None, d, d), (layer, 0, 0)),
                  vec(d), vec(d),
                  _const_spec((d, d_ff), (0, 0)),
                  _const_spec((d_ff, d), (0, 0)),
                  vec(d)],
        out_specs=tok(d),
        out_shape=jax.ShapeDtypeStruct((b, s, d), F32),
        compiler_params=_params(),
        name=f"mix_mlp_s{s}",
    )(x, gb, z, z, z, yf, conv_w, g_co, g_fo, head_mean, w_out, g_post, g_pre2, w_up, w_down, g_post2)


def _cos_sin(rows, cols, period):
    m = np.outer(np.arange(rows), np.arange(cols)) % period
    theta = m * (2.0 * np.pi / period)
    return np.cos(theta), np.sin(theta)


def _seq_dft_tables(s, n1):
    assert s % n1 == 0, (s, n1)
    n2 = s // n1
    ca, sa = (t * (1.0 / math.sqrt(n1)) for t in _cos_sin(n1, n1, n1))
    cb, sb = _cos_sin(n2, n1, s)
    interleave = lambda re, im: jnp.asarray(np.stack([re, im], axis=1).reshape(2 * n1, n1), F32)
    a_cos, a_sin = interleave(ca, -sa), interleave(-sa, -ca)
    b_cos, b_sin = (jnp.asarray(np.repeat(t, 2, axis=1), F32)[:, :, None] for t in (cb, sb))
    l1 = (a_cos[None] * b_cos + a_sin[None] * b_sin).astype(BF16)
    c2, s2 = _cos_sin(n2, n2, n2)
    l2 = np.stack([np.stack([c2, s2], axis=-1), np.stack([-s2, c2], axis=-1)], axis=0)
    l2 = jnp.asarray(l2.reshape(2 * n2, 2 * n2) * (1.0 / math.sqrt(n2)), BF16)
    return l1, l2


def _channel_tables(group):
    same = np.equal.outer(np.arange(group) // HEAD_DIM, np.arange(group) // HEAD_DIM)
    c, sn = _cos_sin(group, group, HEAD_DIM)
    scale = 1.0 / math.sqrt(HEAD_DIM)
    cc = jnp.asarray(np.where(same, c * scale, 0.0), BF16)
    cs = jnp.asarray(np.where(same, sn * scale, 0.0), BF16)
    head_mean = jnp.asarray(np.where(same, 1.0 / HEAD_DIM, 0.0), BF16)
    return cc, cs, head_mean


def _trunk(x, params, tables, ff_f32=None, ff_bf16=None):
    g_mix_pre, w_in, conv_w, g_conv_out, g_fourier_out, w_out, g_mix_post, g_mlp_pre, g_mlp_post = params
    l1, l2, cc, cs, head_mean = tables
    depth = w_in.shape[0]
    converted = []
    for layer in range(depth):
        if ff_f32 is not None:
            gb, z, u, w_up, w_down = _inproj(x, g_mix_pre, w_in, layer, l1.shape[0], ff_f32)
            converted.append((w_up, w_down))
        else:
            gb, z, u = _inproj(x, g_mix_pre, w_in, layer, l1.shape[0])
            w_up, w_down = ff_bf16[layer]
        yf = _seq_dft(u, l1, l2, cc, cs)
        x = _mix_mlp(x, gb, z, yf, conv_w, g_conv_out, g_fourier_out, head_mean, w_out,
                     g_mix_post, g_mlp_pre, w_up, w_down, g_mlp_post, layer)
    return x, converted


def kernel(x_prompt, x_sample, g_mix_pre, w_in, conv_w, g_conv_out, g_fourier_out, w_out,
           g_mix_post, g_mlp_pre, w_up, w_down, g_mlp_post):
    depth = w_in.shape[0]
    vec3 = lambda g: g.reshape(depth, 1, g.shape[-1])
    params = (vec3(g_mix_pre), w_in.astype(BF16), conv_w, vec3(g_conv_out), vec3(g_fourier_out),
              w_out.astype(BF16), vec3(g_mix_post), vec3(g_mlp_pre), vec3(g_mlp_post))
    cc, cs, head_mean = _channel_tables(V7X_MXU_DIM)
    tables = lambda x: (*_seq_dft_tables(x.shape[1], SEQ_DFT_N1), cc, cs, head_mean)
    y_prompt, ff_bf16 = _trunk(x_prompt, params, tables(x_prompt), ff_f32=(w_up, w_down))
    y_sample, _ = _trunk(x_sample, params, tables(x_sample), ff_bf16=ff_bf16)
    return y_prompt, y_sample
```

```python
import functools
import math

import jax
import jax.numpy as jnp
import numpy as np
from jax import lax
from jax.experimental import pallas as pl
from jax.experimental.pallas import tpu as pltpu

HEAD_DIM = 64
EPS = 1e-6
F32 = jnp.float32
BF16 = jnp.bfloat16
U32 = jnp.uint32

V7X_LANES = 128
V7X_SUBLANES = 8
V7X_MXU_DIM = 256
VMEM_LIMIT_BYTES = 56 * 1024 * 1024

SEQ_DFT_N1 = 128
INPROJ_TILE = 2048
INPROJ_SUB_ROWS = 256
MIX_TILE = 1024
MIX_SUB_ROWS = 256
DFT_CHUNK = V7X_SUBLANES
DFT_MAX_CHUNKS_PER_STEP = 4
FF_CHUNK = 1024


def _rms(x, g):
    ms = jnp.mean(x * x, axis=-1, keepdims=True)
    return x * lax.rsqrt(ms + EPS) * g


def _const_spec(block_shape, index):
    return pl.BlockSpec(block_shape, lambda *_: index, pipeline_mode=pl.Buffered(1))


def _params():
    return pltpu.CompilerParams(dimension_semantics=("arbitrary", "arbitrary"),
                                vmem_limit_bytes=VMEM_LIMIT_BYTES)


def _lane_tile(x, t):
    return x[:, t * V7X_LANES:(t + 1) * V7X_LANES]


def _inproj_body(x_ref, g_ref, w_ref, *refs, dc, n2):
    n_cast = (len(refs) - 3) // 2
    gb_ref, z_ref, u_ref = refs[n_cast:n_cast + 3]
    r = INPROJ_SUB_ROWS
    n_chunks, half, _, _ = u_ref.shape
    c = DFT_CHUNK
    rows = (r // n2) * c

    def norm(s):
        return _rms(x_ref[s * r:(s + 1) * r, :], g_ref[...]).astype(BF16)

    def emit(s, p):
        gb_ref[s * r:(s + 1) * r, :] = p[:, :dc]
        z_ref[s * r:(s + 1) * r, :] = p[:, dc:2 * dc] * p[:, 2 * dc:3 * dc]
        bits = pltpu.bitcast(p[:, 3 * dc:].astype(BF16).astype(F32), U32)
        for t in range(half):
            word = _lane_tile(bits, t) | (_lane_tile(bits, t + half) >> 16)
            word = word.reshape(rows // c, n_chunks, c, V7X_LANES)
            for j in range(n_chunks):
                u_ref[j, t, s * rows:(s + 1) * rows, :] = word[:, j].reshape(rows, V7X_LANES)

    h = norm(0)
    for s in range(x_ref.shape[0] // r):
        p = jnp.dot(h, w_ref[...], preferred_element_type=F32)
        if (s + 1) * r < x_ref.shape[0]:
            h = norm(s + 1)
        emit(s, p)
    for k in range(n_cast):
        refs[n_cast + 3 + k][...] = refs[k][...].astype(BF16)


def _inproj(x, g, w, layer, n2, ff_f32=None):
    b, s, d = x.shape
    dc = w.shape[-1] // 4
    t = INPROJ_TILE
    n_lt = dc // V7X_LANES
    c = DFT_CHUNK
    n_chunks = n2 // c
    rows = (t // n2) * c
    assert s % t == 0 and t % INPROJ_SUB_ROWS == 0 and INPROJ_SUB_ROWS % n2 == 0 and n2 % c == 0, (s, n2)
    tok = lambda width: pl.BlockSpec((None, t, width), lambda bi, i: (bi, i, 0))
    out = jax.ShapeDtypeStruct((b, s, dc), F32)
    in_specs = [tok(d),
                _const_spec((None, 1, d), (layer, 0, 0)),
                _const_spec((None, d, 4 * dc), (layer, 0, 0))]
    out_specs = [tok(dc), tok(dc),
                 pl.BlockSpec((None, n_chunks, n_lt // 2, rows, V7X_LANES), lambda bi, i: (bi, 0, 0, i, 0))]
    out_shape = [out, out, jax.ShapeDtypeStruct((b, n_chunks, n_lt // 2, (s // n2) * c, V7X_LANES), U32)]
    operands = [x, g, w]
    if ff_f32 is not None:
        n_t = s // t
        steps = b * n_t
        for wf in ff_f32:
            _, rows_w, cols_w = wf.shape
            assert rows_w % (steps * 2 * V7X_SUBLANES) == 0, (wf.shape, steps)
            in_specs.append(pl.BlockSpec((None, rows_w // steps, cols_w), lambda bi, i: (layer, bi * n_t + i, 0)))
            out_specs.append(pl.BlockSpec((rows_w // steps, cols_w), lambda bi, i: (bi * n_t + i, 0)))
            out_shape.append(jax.ShapeDtypeStruct((rows_w, cols_w), BF16))
            operands.append(wf)
    return pl.pallas_call(
        functools.partial(_inproj_body, dc=dc, n2=n2),
        grid=(b, s // t),
        in_specs=in_specs,
        out_specs=out_specs,
        out_shape=out_shape,
        compiler_params=_params(),
        name=f"inproj_s{s}",
    )(*operands)


def _seq_dft_body(u_ref, l1_ref, l2_ref, cc_ref, cs_ref, y_ref, g_scr, *, n1, n2, c, p):
    j = pl.program_id(1)
    steps1 = n2 // (c * p)
    half = u_ref.shape[1]
    n_lt = 2 * half

    @pl.when(j < steps1)
    def _():
        for q in range(p):
            for s in range(c):
                words = [u_ref[q, t, pl.ds(s, n1, stride=c), :] for t in range(half)]
                slab = jnp.concatenate([pltpu.bitcast(w & jnp.uint32(0xFFFF0000), F32) for w in words]
                                       + [pltpu.bitcast(w << 16, F32) for w in words], axis=1)
                o = jnp.dot(l1_ref[q * c + s], slab.astype(BF16), preferred_element_type=F32)
                packed = pltpu.bitcast(o.astype(BF16), U32)
                for t in range(n_lt):
                    g_scr[t, j * p + q, pl.ds(s, n1, stride=c), :] = _lane_tile(packed, t)

    @pl.when(j >= steps1)
    def _():
        k0 = (j - steps1) * (p * c)
        group = cc_ref.shape[0]
        tiles = group // V7X_LANES
        for q in range(p):
            vr, vi = [], []
            for k in range(c):
                rows = pl.ds((k0 + q * c + k) * c, c)
                w = jnp.concatenate([g_scr[t, :, rows, :].reshape(n2, V7X_LANES) for t in range(n_lt)], axis=1)
                v = jnp.dot(l2_ref[...], pltpu.bitcast(w, BF16), preferred_element_type=F32)
                vr.append(v[:n2].astype(BF16))
                vi.append(v[n2:].astype(BF16))
            vr = jnp.concatenate(vr, axis=0)
            vi = jnp.concatenate(vi, axis=0)
            for g in range(n_lt // tiles):
                cols = slice(g * group, (g + 1) * group)
                y = (jnp.dot(vr[:, cols], cc_ref[...], preferred_element_type=F32)
                     + jnp.dot(vi[:, cols], cs_ref[...], preferred_element_type=F32))
                for k in range(c):
                    yk = y[k * n2:(k + 1) * n2]
                    for t in range(tiles):
                        y_ref[q, g * tiles + t, pl.ds(k, n2, stride=c), :] = _lane_tile(yk, t)


def _dft_chunks_per_step(n1, n2, width):
    c = DFT_CHUNK
    g_bytes = n1 * n2 * width * 4
    for p in range(DFT_MAX_CHUNKS_PER_STEP, 1, -1):
        io_bytes = 2 * p * c * ((n1 // 2 + n2) * width * 4 + 2 * n1 * n1 * 2)
        temp_bytes = p * c * n2 * width * (2 + 2 + 4) + 2 * n1 * width * 4 * 2
        if g_bytes + io_bytes + temp_bytes <= VMEM_LIMIT_BYTES and n2 % (p * c) == 0 and n1 % (p * c) == 0:
            return p
    return 1


def _seq_dft(u, l1, l2, cc, cs):
    b, n_chunks, half, _, _ = u.shape
    n_lt = 2 * half
    n2, _, n1 = l1.shape
    c = DFT_CHUNK
    p = _dft_chunks_per_step(n1, n2, n_lt * V7X_LANES)
    assert n_chunks * c == n2 and n2 % (c * p) == 0 and n1 % (c * p) == 0, (n1, n2, p)
    steps1 = n_chunks // p
    steps2 = n1 // (c * p)
    return pl.pallas_call(
        functools.partial(_seq_dft_body, n1=n1, n2=n2, c=c, p=p),
        grid=(b, steps1 + steps2),
        in_specs=[pl.BlockSpec((None, p, half, n1 * c, V7X_LANES),
                               lambda bi, j: (bi, jnp.minimum(j, steps1 - 1), 0, 0, 0)),
                  pl.BlockSpec((p * c, 2 * n1, n1), lambda bi, j: (jnp.minimum(j, steps1 - 1), 0, 0)),
                  _const_spec(l2.shape, (0, 0)),
                  _const_spec(cc.shape, (0, 0)),
                  _const_spec(cs.shape, (0, 0))],
        out_specs=pl.BlockSpec((None, p, n_lt, n2 * c, V7X_LANES),
                               lambda bi, j: (bi, jnp.maximum(j - steps1, 0), 0, 0, 0)),
        out_shape=jax.ShapeDtypeStruct((b, n1 // c, n_lt, n2 * c, V7X_LANES), F32),
        scratch_shapes=[pltpu.VMEM((n_lt, n_chunks, n1 * c, V7X_LANES), U32)],
        compiler_params=_params(),
        name=f"seq_dft_s{n1 * n2}",
    )(u, l1, l2, cc, cs)


def _mix_mlp_body(x_ref, gb_ref, z_ref, zp_ref, zn_ref, yf_ref, cw_ref, gco_ref, gfo_ref,
                  hm_ref, wo_ref, gpost_ref, gpre2_ref, wu_ref, wd_ref, gpost2_ref, o_ref,
                  *, n_tiles, sub_rows, ff_chunk):
    i = pl.program_id(1)
    r = sub_rows
    n_sub = z_ref.shape[0] // r
    n_ff = wu_ref.shape[-1] // ff_chunk
    halo = V7X_SUBLANES

    def head_norm(y, g):
        sq = (y * y).astype(BF16)
        group = hm_ref.shape[0]
        ms = jnp.concatenate(
            [jnp.dot(sq[:, q * group:(q + 1) * group], hm_ref[...], preferred_element_type=F32)
             for q in range(y.shape[1] // group)], axis=1)
        return y * lax.rsqrt(ms + EPS) * g

    def mix_inputs(s):
        rows = slice(s * r, (s + 1) * r)
        z = z_ref[rows, :]
        row = lax.broadcasted_iota(jnp.int32, z.shape, 0)
        if s == 0:
            z_before = jnp.where(i > 0, zp_ref[halo - 1:halo, :], 0.0)
        else:
            z_before = z_ref[s * r - 1:s * r, :]
        if s == n_sub - 1:
            z_after = jnp.where(i < n_tiles - 1, zn_ref[0:1, :], 0.0)
        else:
            z_after = z_ref[(s + 1) * r:(s + 1) * r + 1, :]
        z_m1 = jnp.where(row == 0, z_before, pltpu.roll(z, 1, 0))
        z_p1 = jnp.where(row == r - 1, z_after, pltpu.roll(z, r - 1, 0))
        cw = cw_ref[...]
        conv = z_m1 * cw[0:1] + z * cw[1:2] + z_p1 * cw[2:3]
        yc = gb_ref[rows, :] * conv
        n_kchunks, n_lt = yf_ref.shape[:2]
        c = DFT_CHUNK
        k2_per_sub = r // (n_kchunks * c)
        yf = jnp.concatenate(
            [jnp.concatenate([yf_ref[kc, t, k2 * c:(k2 + 1) * c, :]
                              for k2 in range(s * k2_per_sub, (s + 1) * k2_per_sub)
                              for kc in range(n_kchunks)], axis=0)
             for t in range(n_lt)], axis=1)
        merged = jnp.concatenate([head_norm(yc, gco_ref[...]), head_norm(yf, gfo_ref[...])], axis=-1)
        return merged.astype(BF16)

    def mix_out(s, merged):
        m = jnp.dot(merged, wo_ref[...], preferred_element_type=F32)
        x1 = x_ref[s * r:(s + 1) * r, :] + _rms(m, gpost_ref[...])
        return x1, _rms(x1, gpre2_ref[...]).astype(BF16)

    def mlp_chunk(h2, c):
        cols = slice(c * ff_chunk, (c + 1) * ff_chunk)
        a = jnp.maximum(jnp.dot(h2, wu_ref[:, cols], preferred_element_type=F32), 0.0)
        return jnp.dot((a * a).astype(BF16), wd_ref[cols, :], preferred_element_type=F32)

    x1, h2 = mix_out(0, mix_inputs(0))
    for s in range(n_sub):
        f = mlp_chunk(h2, 0)
        if s + 1 < n_sub:
            merged = mix_inputs(s + 1)
        f = f + mlp_chunk(h2, 1)
        if s + 1 < n_sub:
            nxt = mix_out(s + 1, merged)
        for c in range(2, n_ff):
            f = f + mlp_chunk(h2, c)
        o_ref[s * r:(s + 1) * r, :] = x1 + _rms(f, gpost2_ref[...])
        if s + 1 < n_sub:
            x1, h2 = nxt


def _mix_mlp(x, gb, z, yf, conv_w, g_co, g_fo, head_mean, w_out, g_post, g_pre2, w_up, w_down,
             g_post2, layer):
    b, s, d = x.shape
    dc = gb.shape[-1]
    d_ff = w_up.shape[-1]
    t = MIX_TILE
    n_tiles = s // t
    halo = V7X_SUBLANES
    hpt = t // halo
    n_halo = s // halo
    n_kchunks, n_lt = yf.shape[1:3]
    yf_rows = t // n_kchunks
    assert s % t == 0 and t % MIX_SUB_ROWS == 0 and MIX_SUB_ROWS % (n_kchunks * DFT_CHUNK) == 0, (s, n_kchunks)
    tok = lambda width: pl.BlockSpec((None, t, width), lambda bi, i: (bi, i, 0))
    vec = lambda width: _const_spec((None, 1, width), (layer, 0, 0))
    return pl.pallas_call(
        functools.partial(_mix_mlp_body, n_tiles=n_tiles, sub_rows=MIX_SUB_ROWS, ff_chunk=FF_CHUNK),
        grid=(b, n_tiles),
        in_specs=[tok(d), tok(dc), tok(dc),
                  pl.BlockSpec((None, halo, dc), lambda bi, i: (bi, jnp.maximum(i * hpt - 1, 0), 0)),
                  pl.BlockSpec((None, halo, dc), lambda bi, i: (bi, jnp.minimum((i + 1) * hpt, n_halo - 1), 0)),
                  pl.BlockSpec((None, n_kchunks, n_lt, yf_rows, V7X_LANES), lambda bi, i: (bi, 0, 0, i, 0)),
                  _const_spec((None, 3, dc), (layer, 0, 0)),
                  vec(dc), vec(dc),
                  _const_spec(head_mean.shape, (0, 0)),
                  _const_spec((d, d), (0, 0)),
                  vec(d), vec(d),
                  _const_spec((d, d_ff), (0, 0)),
                  _const_spec((d_ff, d), (0, 0)),
                  vec(d)],
        out_specs=tok(d),
        out_shape=jax.ShapeDtypeStruct((b, s, d), F32),
        compiler_params=_params(),
        name=f"mix_mlp_s{s}",
    )(x, gb, z, z, z, yf, conv_w, g_co, g_fo, head_mean, w_out, g_post, g_pre2, w_up, w_down, g_post2)


def _cos_sin(rows, cols, period):
    m = np.outer(np.arange(rows), np.arange(cols)) % period
    theta = m * (2.0 * np.pi / period)
    return np.cos(theta), np.sin(theta)


def _seq_dft_tables(s, n1):
    assert s % n1 == 0, (s, n1)
    n2 = s // n1
    ca, sa = (t * (1.0 / math.sqrt(n1)) for t in _cos_sin(n1, n1, n1))
    cb, sb = _cos_sin(n2, n1, s)
    interleave = lambda re, im: jnp.asarray(np.stack([re, im], axis=1).reshape(2 * n1, n1), F32)
    a_cos, a_sin = interleave(ca, -sa), interleave(-sa, -ca)
    b_cos, b_sin = (jnp.asarray(np.repeat(t, 2, axis=1), F32)[:, :, None] for t in (cb, sb))
    l1 = (a_cos[None] * b_cos + a_sin[None] * b_sin).astype(BF16)
    c2, s2 = _cos_sin(n2, n2, n2)
    l2 = np.stack([np.stack([c2, s2], axis=-1), np.stack([-s2, c2], axis=-1)], axis=0)
    l2 = jnp.asarray(l2.reshape(2 * n2, 2 * n2) * (1.0 / math.sqrt(n2)), BF16)
    return l1, l2


def _channel_tables(group):
    same = np.equal.outer(np.arange(group) // HEAD_DIM, np.arange(group) // HEAD_DIM)
    c, sn = _cos_sin(group, group, HEAD_DIM)
    scale = 1.0 / math.sqrt(HEAD_DIM)
    cc = jnp.asarray(np.where(same, c * scale, 0.0), BF16)
    cs = jnp.asarray(np.where(same, sn * scale, 0.0), BF16)
    head_mean = jnp.asarray(np.where(same, 1.0 / HEAD_DIM, 0.0), BF16)
    return cc, cs, head_mean


def _trunk(x, params, tables, ff_f32=None, ff_bf16=None):
    g_mix_pre, w_in, conv_w, g_conv_out, g_fourier_out, g_mix_post, g_mlp_pre, g_mlp_post = params
    l1, l2, cc, cs, head_mean = tables
    depth = w_in.shape[0]
    converted = []
    for layer in range(depth):
        if ff_f32 is not None:
            gb, z, u, w_up, w_down, w_out = _inproj(x, g_mix_pre, w_in, layer, l1.shape[0], ff_f32)
            converted.append((w_up, w_down, w_out))
        else:
            gb, z, u = _inproj(x, g_mix_pre, w_in, layer, l1.shape[0])
            w_up, w_down, w_out = ff_bf16[layer]
        yf = _seq_dft(u, l1, l2, cc, cs)
        x = _mix_mlp(x, gb, z, yf, conv_w, g_conv_out, g_fourier_out, head_mean, w_out,
                     g_mix_post, g_mlp_pre, w_up, w_down, g_mlp_post, layer)
    return x, converted


def kernel(x_prompt, x_sample, g_mix_pre, w_in, conv_w, g_conv_out, g_fourier_out, w_out,
           g_mix_post, g_mlp_pre, w_up, w_down, g_mlp_post):
    depth = w_in.shape[0]
    vec3 = lambda g: g.reshape(depth, 1, g.shape[-1])
    params = (vec3(g_mix_pre), w_in.astype(BF16), conv_w, vec3(g_conv_out), vec3(g_fourier_out),
              vec3(g_mix_post), vec3(g_mlp_pre), vec3(g_mlp_post))
    cc, cs, head_mean = _channel_tables(V7X_MXU_DIM)
    tables = lambda x: (*_seq_dft_tables(x.shape[1], SEQ_DFT_N1), cc, cs, head_mean)
    y_prompt, ff_bf16 = _trunk(x_prompt, params, tables(x_prompt), ff_f32=(w_up, w_down, w_out))
    y_sample, _ = _trunk(x_sample, params, tables(x_sample), ff_bf16=ff_bf16)
    return y_prompt, y_sample
```
